```python
import math
import numpy as np
import jax
import jax.numpy as jnp
from jax import lax

D_MODEL = 2048
BATCH = 4
SEQ = 2048
DEPTH = 4

F32 = jnp.float32
NORM_EPS = 1e-5

M_HEADS = 16
M_HEAD_DIM = 64
M_WIDTH = M_HEADS * M_HEAD_DIM
M_GROUPS = 4
M_STATE = 128
M_XBC = M_WIDTH + 2 * M_GROUPS * M_STATE
M_CONV = 4
M_CHUNK = 64

H_HEADS = 8
H_KEY = 128
H_VAL = 128
H_WIDTH = H_HEADS * H_VAL
H_CHUNK = 16

DN_HEADS = 8
DN_KEY = 128
DN_VAL = 128
DN_WIDTH = DN_HEADS * DN_VAL
DN_CONV = 4
DN_CHUNK = 64

G_HEADS = 4
G_KEY = 128
G_VAL = 256
G_WIDTH = G_HEADS * G_VAL
G_GATE_RANK = 16
G_GATE_TEMP = 16.0
G_CHUNK = 64

N_BRANCH = 4
BRANCH_WIDTH = 1024

IN_SIZES = (M_WIDTH, M_XBC, M_HEADS,
            H_HEADS * H_KEY, H_HEADS * H_KEY, H_WIDTH, H_WIDTH,
            DN_HEADS * DN_KEY, DN_HEADS * DN_KEY, DN_WIDTH, DN_HEADS, DN_HEADS, DN_WIDTH,
            G_HEADS * G_KEY, G_HEADS * G_KEY, G_WIDTH, G_WIDTH, G_GATE_RANK)
IN_WIDTH = sum(IN_SIZES)

P_HEADS = 8
P_NKEYS = 128
P_EXPERTS = P_NKEYS * P_NKEYS
P_QDIM = 256
P_HALF = P_QDIM // 2
P_TOPK = 16
P_TOKEN_BLOCK = 128

ALPHA = (2 * DEPTH) ** 0.25
BETA = (8 * DEPTH) ** -0.25

kernel_name = "hybrid_ssd_hgrn2_gdn_gla_peer"


def _rms(x):
    return x * lax.rsqrt(jnp.mean(x * x, axis=-1, keepdims=True) + NORM_EPS)


def layer_norm(x, gain, bias):
    xf = x.astype(F32)
    xc = xf - jnp.mean(xf, axis=-1, keepdims=True)
    y = xc * lax.rsqrt(jnp.mean(xc * xc, axis=-1, keepdims=True) + NORM_EPS)
    return (y * gain.astype(F32) + bias.astype(F32)).astype(x.dtype)


def causal_dwconv(x, w, b=None):
    k, c = w.shape
    y = lax.conv_general_dilated(x, w.astype(x.dtype)[:, None, :], window_strides=(1,),
                                 padding=[(k - 1, 0)], dimension_numbers=("NWC", "WIO", "NWC"),
                                 feature_group_count=c)
    if b is not None:
        y = y + b.astype(y.dtype)
    return y


def _masked_decay(diff, mask):
    return jnp.where(mask, jnp.exp(jnp.minimum(diff, 0.0)), 0.0)


def chunked_gla(q, k, v, log_g, chunk):
    bsz, s, nh, dk = q.shape
    dv = v.shape[-1]
    nc = s // chunk
    q, k, v, log_g = (t.reshape(bsz, nc, chunk, nh, t.shape[-1]) for t in (q, k, v, log_g))
    b = jnp.cumsum(log_g, axis=2)
    mid = chunk // 2
    b_mid = b[:, :, mid - 1:mid]
    causal = jnp.tril(jnp.ones((chunk, chunk), bool))
    att = jnp.where(causal, jnp.einsum("bclhk,bcshk->bchls", q * jnp.exp(b - b_mid),
                                       k * jnp.exp(b_mid - b)), 0.0)
    o_intra = jnp.einsum("bchls,bcshv->bclhv", att, v)
    q_dec = q * jnp.exp(b)
    k_end = k * jnp.exp(b[:, :, -1:] - b)
    kv = jnp.einsum("bclhk,bclhv->bchkv", k_end, v)
    g_end = jnp.exp(b[:, :, -1])

    def step(state, inp):
        kv_c, g_c = inp
        return g_c[..., None] * state + kv_c, state

    _, s_in = lax.scan(step, jnp.zeros((bsz, nh, dk, dv), F32),
                       (jnp.moveaxis(kv, 1, 0), jnp.moveaxis(g_end, 1, 0)))
    o_inter = jnp.einsum("bclhk,bchkv->bclhv", q_dec, jnp.moveaxis(s_in, 0, 1))
    return (o_intra + o_inter).reshape(bsz, s, nh, dv)


def ssd_branch(z, xbc, dt_raw, conv_w, conv_b, dt_bias, a_log, d_skip, norm_w):
    bsz, s, _ = z.shape
    L = M_CHUNK
    nc = s // L
    hg = M_HEADS // M_GROUPS
    xbc = jax.nn.silu(causal_dwconv(xbc, conv_w, conv_b)).astype(F32)
    xs, bm, cm = jnp.split(xbc, [M_WIDTH, M_WIDTH + M_GROUPS * M_STATE], axis=-1)
    x = xs.reshape(bsz, nc, L, M_GROUPS, hg, M_HEAD_DIM)
    bm = bm.reshape(bsz, nc, L, M_GROUPS, M_STATE)
    cm = cm.reshape(bsz, nc, L, M_GROUPS, M_STATE)
    dt = jax.nn.softplus(dt_raw.astype(F32) + dt_bias.astype(F32)).reshape(bsz, nc, L, M_GROUPS, hg)
    a = -jnp.exp(a_log.astype(F32)).reshape(M_GROUPS, hg)
    a_cs = jnp.cumsum(dt * a, axis=2)
    xdt = x * dt[..., None]
    causal = jnp.tril(jnp.ones((L, L), bool))[:, :, None, None]
    seg = _masked_decay(a_cs[:, :, :, None] - a_cs[:, :, None, :], causal)
    cb = jnp.einsum("bclgn,bcsgn->bclsg", cm, bm)
    y = jnp.einsum("bclsg,bclsgh,bcsghp->bclghp", cb, seg, xdt)
    states = jnp.einsum("bclgn,bclgh,bclghp->bcghpn", bm, jnp.exp(a_cs[:, :, -1:] - a_cs), xdt)

    def step(h, inp):
        st, dec = inp
        return dec[..., None, None] * h + st, h

    _, h_in = lax.scan(step, jnp.zeros((bsz, M_GROUPS, hg, M_HEAD_DIM, M_STATE), F32),
                       (jnp.moveaxis(states, 1, 0), jnp.moveaxis(jnp.exp(a_cs[:, :, -1]), 1, 0)))
    y = y + jnp.einsum("bclgn,bcghpn,bclgh->bclghp", cm, jnp.moveaxis(h_in, 0, 1), jnp.exp(a_cs))
    y = y + x * d_skip.astype(F32).reshape(M_GROUPS, hg)[:, :, None]
    y = y.reshape(bsz, s, M_WIDTH) * jax.nn.silu(z.astype(F32))
    y = _rms(y.reshape(bsz, s, M_GROUPS, M_WIDTH // M_GROUPS)).reshape(bsz, s, M_WIDTH)
    return (y * norm_w.astype(F32)).astype(z.dtype)


def hgrn2_branch(q, f, i, g, lower_bound, norm_w):
    bsz, s, _ = q.shape
    pre = f.astype(F32)
    lb = lower_bound.astype(F32)
    forget = lb + (1.0 - lb) * jax.nn.sigmoid(pre)
    log_f = jnp.log(forget)
    k = (1.0 - lb) * jax.nn.sigmoid(-pre)
    shp = (bsz, s, H_HEADS, H_KEY)
    o = chunked_gla(jax.nn.silu(q.astype(F32)).reshape(shp), k.reshape(shp),
                    i.astype(F32).reshape(bsz, s, H_HEADS, H_VAL), log_f.reshape(shp), H_CHUNK)
    o = _rms(o) * norm_w.astype(F32)
    return (o.reshape(bsz, s, H_WIDTH) * jax.nn.sigmoid(g.astype(F32))).astype(q.dtype)


def gdn_branch(q, k, v, a, b, g, conv_w, a_log, dt_bias, norm_w):
    bsz, s, _ = q.shape
    L = DN_CHUNK
    nc = s // L
    qkv = jax.nn.silu(causal_dwconv(jnp.concatenate([q, k, v], axis=-1), conv_w)).astype(F32)
    q, k, v = jnp.split(qkv, [DN_HEADS * DN_KEY, 2 * DN_HEADS * DN_KEY], axis=-1)

    def chunk(t, dim):
        return jnp.swapaxes(t.reshape(bsz, nc, L, DN_HEADS, dim), 2, 3)

    q, k, v = chunk(q, DN_KEY), chunk(k, DN_KEY), chunk(v, DN_VAL)
    q = q * lax.rsqrt(jnp.sum(q * q, axis=-1, keepdims=True) + 1e-6) * DN_KEY ** -0.5
    k = k * lax.rsqrt(jnp.sum(k * k, axis=-1, keepdims=True) + 1e-6)
    beta = jnp.swapaxes(jax.nn.sigmoid(b.astype(F32)).reshape(bsz, nc, L, DN_HEADS), 2, 3)
    log_alpha = -jnp.exp(a_log.astype(F32)) * jax.nn.softplus(a.astype(F32) + dt_bias.astype(F32))
    decay = jnp.cumsum(jnp.swapaxes(log_alpha.reshape(bsz, nc, L, DN_HEADS), 2, 3), axis=-1)
    incl = jnp.tril(jnp.ones((L, L), bool))
    strict = jnp.tril(jnp.ones((L, L), bool), -1)
    lmask = _masked_decay(decay[..., :, None] - decay[..., None, :], incl)
    k_beta = k * beta[..., None]
    a_mat = jnp.where(strict, jnp.einsum("bchlk,bchsk->bchls", k_beta, k) * lmask, 0.0)
    rhs = jnp.concatenate([k_beta * jnp.exp(decay)[..., None], v * beta[..., None]], axis=-1)
    sol = lax.linalg.triangular_solve(a_mat + jnp.eye(L, dtype=F32), rhs, left_side=True,
                                      lower=True, unit_diagonal=True)
    k_cum, u = jnp.split(sol, [DN_KEY], axis=-1)
    att = jnp.einsum("bchlk,bchsk->bchls", q, k) * lmask
    q_dec = q * jnp.exp(decay)[..., None]
    k_end = k * jnp.exp(decay[..., -1:] - decay)[..., None]
    g_end = jnp.exp(decay[..., -1])

    def step(state, inp):
        qd, kc, uc, ac, ke, ge = inp
        v_new = uc - jnp.einsum("bhlk,bhkv->bhlv", kc, state)
        o = jnp.einsum("bhlk,bhkv->bhlv", qd, state) + jnp.einsum("bhls,bhsv->bhlv", ac, v_new)
        state = ge[..., None, None] * state + jnp.einsum("bhlk,bhlv->bhkv", ke, v_new)
        return state, o

    xs = tuple(jnp.moveaxis(t, 1, 0) for t in (q_dec, k_cum, u, att, k_end, g_end))
    _, o = lax.scan(step, jnp.zeros((bsz, DN_HEADS, DN_KEY, DN_VAL), F32), xs)
    o = jnp.transpose(o, (1, 0, 3, 2, 4)).reshape(bsz, s, DN_HEADS, DN_VAL)
    o = _rms(o) * norm_w.astype(F32)
    return (o.reshape(bsz, s, DN_WIDTH) * jax.nn.silu(g.astype(F32))).astype(g.dtype)


def gla_branch(q, k, v, g, gate_lr, gate_w, gate_b, norm_w):
    bsz, s, _ = q.shape
    log_a = jax.nn.log_sigmoid(gate_lr.astype(F32) @ gate_w.astype(F32) + gate_b.astype(F32)) / G_GATE_TEMP
    shp = (bsz, s, G_HEADS, G_KEY)
    o = chunked_gla(q.astype(F32).reshape(shp) * G_KEY ** -0.5, k.astype(F32).reshape(shp),
                    v.astype(F32).reshape(bsz, s, G_HEADS, G_VAL), log_a.reshape(shp), G_CHUNK)
    o = _rms(o) * norm_w.astype(F32)
    return (o.reshape(bsz, s, G_WIDTH) * jax.nn.silu(g.astype(F32))).astype(g.dtype)


def peer_ffn(x, w_query, sub_keys, expert_u, expert_v):
    bsz, s, d = x.shape
    qr = (x @ w_query).astype(F32).reshape(bsz, s, P_HEADS, 2, P_HALF)
    scores = jnp.einsum("bshtc,tnc->bshtn", qr, sub_keys.astype(F32))
    top_v, top_i = lax.top_k(scores, P_TOPK)
    cand_v = (top_v[..., 0, :, None] + top_v[..., 1, None, :]).reshape(bsz, s, P_HEADS, P_TOPK * P_TOPK)
    cand_id = (top_i[..., 0, :, None] * P_NKEYS + top_i[..., 1, None, :]).reshape(bsz, s, P_HEADS, P_TOPK * P_TOPK)
    best_v, best_pos = lax.top_k(cand_v, P_TOPK)
    expert_id = jnp.take_along_axis(cand_id, best_pos, axis=-1)
    gate = jax.nn.softmax(best_v, axis=-1)
    nb = bsz * s // P_TOKEN_BLOCK
    xb = x.reshape(nb, P_TOKEN_BLOCK, d)
    idb = expert_id.reshape(nb, P_TOKEN_BLOCK, P_HEADS * P_TOPK)
    gb = gate.reshape(nb, P_TOKEN_BLOCK, P_HEADS * P_TOPK).astype(x.dtype)

    def block(args):
        xt, ids, gt = args
        u = expert_u[ids]
        vv = expert_v[ids]
        act = jax.nn.gelu(jnp.einsum("td,ted->te", xt, u))
        return jnp.einsum("te,ted->td", gt * act, vv)

    return lax.map(block, (xb, idb, gb)).reshape(bsz, s, d)


def _inv_softplus_dt(key, shape):
    dt = jnp.exp(jax.random.uniform(key, shape, F32, math.log(1e-3), math.log(1e-1)))
    return dt + jnp.log(-jnp.expm1(-dt))


def setup_inputs(seed: int = 0) -> dict:
    key = jax.random.key(seed)
    ks = jax.random.split(key, 32)

    def nrm(k, shape, scale):
        return jax.random.normal(k, shape, F32) * scale

    def gain(k, shape):
        return 1.0 + nrm(k, shape, 0.02)

    return {
        "x": nrm(ks[0], (BATCH, SEQ, D_MODEL), 1.0),
        "w_in": nrm(ks[1], (DEPTH, D_MODEL, IN_WIDTH), D_MODEL ** -0.5),
        "m_conv_w": nrm(ks[2], (DEPTH, M_CONV, M_XBC), M_CONV ** -0.5),
        "m_conv_b": nrm(ks[3], (DEPTH, M_XBC), 0.02),
        "m_dt_bias": _inv_softplus_dt(ks[4], (DEPTH, M_HEADS)),
        "m_a_log": jnp.log(jax.random.uniform(ks[5], (DEPTH, M_HEADS), F32, 1.0, 16.0)),
        "m_d_skip": gain(ks[6], (DEPTH, M_HEADS)),
        "m_norm_w": gain(ks[7], (DEPTH, M_WIDTH)),
        "h_lb_logits": nrm(ks[8], (DEPTH, H_HEADS * H_KEY), 0.1),
        "h_norm_w": gain(ks[9], (DEPTH, H_VAL)),
        "dn_conv_w": nrm(ks[10], (DEPTH, DN_CONV, 2 * DN_HEADS * DN_KEY + DN_WIDTH), DN_CONV ** -0.5),
        "dn_a_log": jnp.log(jax.random.uniform(ks[11], (DEPTH, DN_HEADS), F32, 1.0, 16.0)),
        "dn_dt_bias": _inv_softplus_dt(ks[12], (DEPTH, DN_HEADS)),
        "dn_norm_w": gain(ks[13], (DEPTH, DN_VAL)),
        "g_gate_w": nrm(ks[14], (DEPTH, G_GATE_RANK, G_HEADS * G_KEY), G_GATE_RANK ** -0.5),
        "g_gate_b": nrm(ks[15], (DEPTH, G_HEADS * G_KEY), 0.02),
        "g_norm_w": gain(ks[16], (DEPTH, G_VAL)),
        "w_branch": nrm(ks[17], (DEPTH, N_BRANCH, BRANCH_WIDTH, D_MODEL), BRANCH_WIDTH ** -0.5),
        "w_merge": nrm(ks[18], (DEPTH, D_MODEL, N_BRANCH * D_MODEL), D_MODEL ** -0.5),
        "b_merge": nrm(ks[19], (DEPTH, N_BRANCH * D_MODEL), 0.02),
        "w_out": nrm(ks[20], (DEPTH, D_MODEL, D_MODEL), BETA * D_MODEL ** -0.5),
        "ln1_w": gain(ks[21], (DEPTH, D_MODEL)),
        "ln1_b": nrm(ks[22], (DEPTH, D_MODEL), 0.02),
        "p_w_query": nrm(ks[23], (DEPTH, D_MODEL, P_HEADS * P_QDIM), D_MODEL ** -0.5),
        "p_sub_keys": nrm(ks[24], (DEPTH, 2, P_NKEYS, P_HALF), P_HALF ** -0.5),
        "p_expert_u": nrm(ks[25], (DEPTH, P_EXPERTS, D_MODEL), D_MODEL ** -0.5),
        "p_expert_v": nrm(ks[26], (DEPTH, P_EXPERTS, D_MODEL), BETA * P_HEADS ** -0.5),
        "ln2_w": gain(ks[27], (DEPTH, D_MODEL)),
        "ln2_b": nrm(ks[28], (DEPTH, D_MODEL), 0.02),
    }


def reference(x, w_in, m_conv_w, m_conv_b, m_dt_bias, m_a_log, m_d_skip, m_norm_w,
              h_lb_logits, h_norm_w, dn_conv_w, dn_a_log, dn_dt_bias, dn_norm_w,
              g_gate_w, g_gate_b, g_norm_w, w_branch, w_merge, b_merge, w_out, ln1_w, ln1_b,
              p_w_query, p_sub_keys, p_expert_u, p_expert_v, ln2_w, ln2_b):
    bsz, s, _ = x.shape
    split_at = np.cumsum(IN_SIZES)[:-1].tolist()
    lb_p = jax.nn.softmax(h_lb_logits.astype(F32), axis=0)
    lower_bounds = jnp.cumsum(lb_p, axis=0) - lb_p[0]
    for l in range(DEPTH):
        proj = x @ w_in[l]
        (m_z, m_xbc, m_dt, h_q, h_f, h_i, h_g, d_q, d_k, d_v, d_a, d_b, d_g,
         g_q, g_k, g_v, g_g, g_lr) = jnp.split(proj, split_at, axis=-1)
        y_m = ssd_branch(m_z, m_xbc, m_dt, m_conv_w[l], m_conv_b[l], m_dt_bias[l], m_a_log[l],
                         m_d_skip[l], m_norm_w[l])
        y_h = hgrn2_branch(h_q, h_f, h_i, h_g, lower_bounds[l], h_norm_w[l])
        y_d = gdn_branch(d_q, d_k, d_v, d_a, d_b, d_g, dn_conv_w[l], dn_a_log[l], dn_dt_bias[l],
                         dn_norm_w[l])
        y_g = gla_branch(g_q, g_k, g_v, g_g, g_lr, g_gate_w[l], g_gate_b[l], g_norm_w[l])
        branches = jnp.stack([y_m, y_h, y_d, y_g], axis=2)
        branch_d = jnp.einsum("bsnc,ncd->bsnd", branches, w_branch[l])
        gates = jax.nn.sigmoid((x @ w_merge[l] + b_merge[l]).astype(F32)).reshape(bsz, s, N_BRANCH, D_MODEL)
        merged = jnp.einsum("bsnd,bsnd->bsd", gates.astype(x.dtype), branch_d)
        x = layer_norm(ALPHA * x + merged @ w_out[l], ln1_w[l], ln1_b[l])
        ffn = peer_ffn(x, p_w_query[l], p_sub_keys[l], p_expert_u[l], p_expert_v[l])
        x = layer_norm(ALPHA * x + ffn, ln2_w[l], ln2_b[l])
    return x
```

```python
import functools
import math

import numpy as np
import jax
import jax.numpy as jnp
from jax import lax
from jax.experimental import pallas as pl
from jax.experimental.pallas import tpu as pltpu

F32 = jnp.float32
BF16 = jnp.bfloat16
HIGHEST = lax.Precision.HIGHEST
NORM_EPS = 1e-5

D_MODEL = 2048
DEPTH = 4

M_HEADS, M_HEAD_DIM, M_GROUPS, M_STATE, M_CONV, M_CHUNK = 16, 64, 4, 128, 4, 64
M_WIDTH = M_HEADS * M_HEAD_DIM
M_XBC = M_WIDTH + 2 * M_GROUPS * M_STATE
M_GROUP_WIDTH = M_WIDTH // M_GROUPS

H_HEADS, H_KEY, H_VAL, H_CHUNK = 8, 128, 128, 16
DN_HEADS, DN_KEY, DN_VAL, DN_CONV, DN_CHUNK = 8, 128, 128, 4, 64
G_HEADS, G_KEY, G_VAL, G_GATE_RANK, G_GATE_TEMP, G_CHUNK = 4, 128, 256, 16, 16.0, 64
N_BRANCH, BRANCH_WIDTH = 4, 1024

P_HEADS, P_NKEYS, P_QDIM, P_TOPK = 8, 128, 256, 16
P_HALF = P_QDIM // 2
P_EXPERTS = P_NKEYS * P_NKEYS

ALPHA = (2 * DEPTH) ** 0.25

C_MZ, C_XBC = 0, 1024
C_HQ, C_HF, C_HI, C_HG = 3072, 4096, 5120, 6144
C_DQ, C_DG = 7168, 10240
C_GQ, C_GK, C_GV, C_GG = 11264, 11776, 12288, 13312
BIG_WIDTH = 14336
S_DT, S_DA, S_DB, S_LR = 0, 16, 24, 32
SMALL_WIDTH = 128
LANES = 128

VMEM_LIMIT = 48 * 1024 * 1024


def _cparams(sem):
    return pltpu.CompilerParams(dimension_semantics=sem, vmem_limit_bytes=VMEM_LIMIT)


def _dot(a, b, precision=None):
    return jnp.dot(a, b, preferred_element_type=F32, precision=precision)


def _dot_nt(a, b, precision=None):
    return lax.dot_general(a, b, (((1,), (1,)), ((), ())), preferred_element_type=F32, precision=precision)


def _dot_tn(a, b, precision=None):
    return lax.dot_general(a, b, (((0,), (0,)), ((), ())), preferred_element_type=F32, precision=precision)


def _sigmoid(x):
    return 1.0 / (1.0 + jnp.exp(-x))


def _silu(x):
    return x * _sigmoid(x)


def _softplus(x):
    return jnp.maximum(x, 0.0) + jnp.log(1.0 + jnp.exp(-jnp.abs(x)))


def _tril(n, strict=False):
    r = lax.broadcasted_iota(jnp.int32, (n, n), 0)
    c = lax.broadcasted_iota(jnp.int32, (n, n), 1)
    return (r > c) if strict else (r >= c)


def _matmul_kernel(x_ref, w_ref, o_ref, *, precision):
    o_ref[...] = _dot(x_ref[...], w_ref[...], precision).astype(o_ref.dtype)


def matmul(x, w, *, tm, tn, out_dtype=F32, precision=None):
    m, k = x.shape
    n = w.shape[1]
    return pl.pallas_call(
        functools.partial(_matmul_kernel, precision=precision),
        grid=(n // tn, m // tm),
        in_specs=[pl.BlockSpec((tm, k), lambda j, i: (i, 0)),
                  pl.BlockSpec((k, tn), lambda j, i: (0, j))],
        out_specs=pl.BlockSpec((tm, tn), lambda j, i: (i, j)),
        out_shape=jax.ShapeDtypeStruct((m, n), out_dtype),
        compiler_params=_cparams(("parallel", "parallel")),
        name="matmul",
    )(x, w)


def _conv_kernel(x_ref, halo_ref, w_ref, b_ref, o_ref, *, blocks_per_seq, taps):
    i = pl.program_id(0)
    first = (i % blocks_per_seq) == 0
    tb = x_ref.shape[0]
    halo = jnp.where(first, 0.0, halo_ref[...])
    ext = jnp.concatenate([halo, x_ref[...]], axis=0)
    acc = b_ref[...]
    for j in range(taps):
        off = 8 - (taps - 1) + j
        acc = acc + w_ref[j:j + 1, :] * ext[off:off + tb, :]
    o_ref[...] = _silu(acc).astype(o_ref.dtype)


def conv_silu(proj, w, b, *, col0, seq, tb=256, tc=512, out_dtype=F32):
    t = proj.shape[0]
    taps, width = w.shape
    cb0 = col0 // tc
    hb = tb // 8
    return pl.pallas_call(
        functools.partial(_conv_kernel, blocks_per_seq=seq // tb, taps=taps),
        grid=(t // tb, width // tc),
        in_specs=[pl.BlockSpec((tb, tc), lambda i, j: (i, cb0 + j)),
                  pl.BlockSpec((8, tc), lambda i, j: (jnp.maximum(i * hb - 1, 0), cb0 + j)),
                  pl.BlockSpec((taps, tc), lambda i, j: (0, j)),
                  pl.BlockSpec((1, tc), lambda i, j: (0, j))],
        out_specs=pl.BlockSpec((tb, tc), lambda i, j: (i, j)),
        out_shape=jax.ShapeDtypeStruct((t, width), out_dtype),
        compiler_params=_cparams(("parallel", "parallel")),
        name="conv_silu",
    )(proj, proj, w, b.reshape(1, width))


def _gla_chunks(q, k, v, log_g, st_ref, chunk, emit):
    tb = q.shape[0]
    tril = _tril(chunk)
    tril_f = tril.astype(F32)
    mid = chunk // 2
    st = st_ref[...]
    for c in range(tb // chunk):
        rows = slice(c * chunk, (c + 1) * chunk)
        qc, kc, vc = q[rows], k[rows], v[rows]
        b = _dot(tril_f, log_g[rows], HIGHEST)
        b_mid = b[mid - 1:mid, :]
        b_end = b[chunk - 1:chunk, :]
        qf = (qc * jnp.exp(b - b_mid)).astype(BF16)
        kf = (kc * jnp.exp(b_mid - b)).astype(BF16)
        att = jnp.where(tril, _dot_nt(qf, kf), 0.0)
        vb = vc.astype(BF16)
        o = _dot(att.astype(BF16), vb)
        o = o + _dot_nt((qc * jnp.exp(b)).astype(BF16), st.astype(BF16))
        emit(c, o)
        k_end = (kc * jnp.exp(b_end - b)).astype(BF16)
        st = st * jnp.exp(b_end) + _dot_tn(vb, k_end)
    st_ref[...] = st


def _rms(x):
    return x * lax.rsqrt(jnp.mean(x * x, axis=-1, keepdims=True) + NORM_EPS)


def _hgrn2_kernel(q_ref, f_ref, i_ref, g_ref, lb_ref, nw_ref, o_ref, st_ref, *, chunk):
    @pl.when(pl.program_id(2) == 0)
    def _():
        st_ref[...] = jnp.zeros_like(st_ref)

    lb = lb_ref[...]
    pre = f_ref[...]
    forget = lb + (1.0 - lb) * _sigmoid(pre)
    k = (1.0 - lb) * _sigmoid(-pre)
    gate = _sigmoid(g_ref[...])
    nw = nw_ref[...]

    def emit(c, o):
        rows = slice(c * chunk, (c + 1) * chunk)
        o_ref[rows, :] = (_rms(o) * nw * gate[rows]).astype(o_ref.dtype)

    _gla_chunks(_silu(q_ref[...]), k, i_ref[...], jnp.log(forget), st_ref, chunk, emit)


def hgrn2_branch(proj, lower_bound, norm_w, *, batch, seq, tb=256, out_dtype=BF16):
    t = proj.shape[0]
    nb = seq // tb
    kb = H_KEY // LANES

    def col(c0):
        return pl.BlockSpec((tb, H_KEY), lambda b, h, i: (b * nb + i, c0 // H_KEY + h))

    return pl.pallas_call(
        functools.partial(_hgrn2_kernel, chunk=H_CHUNK),
        grid=(batch, H_HEADS, nb),
        in_specs=[col(C_HQ), col(C_HF), col(C_HI), col(C_HG),
                  pl.BlockSpec((1, H_KEY), lambda b, h, i: (0, h)),
                  pl.BlockSpec((1, H_VAL), lambda b, h, i: (0, 0))],
        out_specs=pl.BlockSpec((tb, H_VAL), lambda b, h, i: (b * nb + i, h)),
        out_shape=jax.ShapeDtypeStruct((t, H_HEADS * H_VAL), out_dtype),
        scratch_shapes=[pltpu.VMEM((H_VAL, H_KEY), F32)],
        compiler_params=_cparams(("parallel", "parallel", "arbitrary")),
        name="hgrn2",
    )(proj, proj, proj, proj, lower_bound.reshape(1, -1), norm_w.reshape(1, -1))


def _gla_kernel(q_ref, k_ref, v_ref, g_ref, lr_ref, gw_ref, gb_ref, nw_ref, o_ref, st_ref, *, chunk):
    @pl.when(pl.program_id(2) == 0)
    def _():
        st_ref[...] = jnp.zeros_like(st_ref)

    z = _dot(lr_ref[...], gw_ref[...], HIGHEST) + gb_ref[...]
    log_a = (jnp.minimum(z, 0.0) - jnp.log(1.0 + jnp.exp(-jnp.abs(z)))) * (1.0 / G_GATE_TEMP)
    gate = _silu(g_ref[...])
    nw = nw_ref[...]

    def emit(c, o):
        rows = slice(c * chunk, (c + 1) * chunk)
        o_ref[rows, :] = (_rms(o) * nw * gate[rows]).astype(o_ref.dtype)

    _gla_chunks(q_ref[...] * G_KEY ** -0.5, k_ref[...], v_ref[...], log_a, st_ref, chunk, emit)


def gla_branch(proj, small, gate_w_pad, gate_b, norm_w, *, batch, seq, tb=256, out_dtype=BF16):
    t = proj.shape[0]
    nb = seq // tb

    def col(c0, w):
        return pl.BlockSpec((tb, w), lambda b, h, i: (b * nb + i, c0 // w + h))

    return pl.pallas_call(
        functools.partial(_gla_kernel, chunk=G_CHUNK),
        grid=(batch, G_HEADS, nb),
        in_specs=[col(C_GQ, G_KEY), col(C_GK, G_KEY), col(C_GV, G_VAL), col(C_GG, G_VAL),
                  pl.BlockSpec((tb, SMALL_WIDTH), lambda b, h, i: (b * nb + i, 0)),
                  pl.BlockSpec((SMALL_WIDTH, G_KEY), lambda b, h, i: (0, h)),
                  pl.BlockSpec((1, G_KEY), lambda b, h, i: (0, h)),
                  pl.BlockSpec((1, G_VAL), lambda b, h, i: (0, 0))],
        out_specs=pl.BlockSpec((tb, G_VAL), lambda b, h, i: (b * nb + i, h)),
        out_shape=jax.ShapeDtypeStruct((t, G_HEADS * G_VAL), out_dtype),
        scratch_shapes=[pltpu.VMEM((G_VAL, G_KEY), F32)],
        compiler_params=_cparams(("parallel", "parallel", "arbitrary")),
        name="gla",
    )(proj, proj, proj, proj, small, gate_w_pad, gate_b.reshape(1, -1), norm_w.reshape(1, -1))


def _ssd_kernel(z_ref, x_ref, b_ref, c_ref, small_ref, dtt_ref, dtb_ref, dtbc_ref, a_ref, ac_ref,
                e_ref, dskip_ref, nw_ref, o_ref, h_ref, *, chunk):
    @pl.when(pl.program_id(2) == 0)
    def _():
        h_ref[...] = jnp.zeros_like(h_ref)

    tb = x_ref.shape[0]
    hg = M_HEADS // M_GROUPS
    p = M_HEAD_DIM
    tril = _tril(chunk)
    tril_f = tril.astype(F32)
    triu_f = (lax.broadcasted_iota(jnp.int32, (chunk, chunk), 0)
              <= lax.broadcasted_iota(jnp.int32, (chunk, chunk), 1)).astype(F32)
    dt = _dot(_softplus(small_ref[...] + dtb_ref[...]), e_ref[...], HIGHEST)
    da = dt * a_ref[...]
    da_t = _softplus(dtt_ref[...] + dtbc_ref[...]) * ac_ref[...]
    x = x_ref[...]
    xdt = x * dt
    zg = _silu(z_ref[...])
    bm = b_ref[...].astype(BF16)
    cm = c_ref[...].astype(BF16)
    dskip = dskip_ref[...]
    nw = nw_ref[...]
    for c in range(tb // chunk):
        rows = slice(c * chunk, (c + 1) * chunk)
        acs = _dot(tril_f, da[rows], HIGHEST)
        acs_t = _dot(da_t[:, rows], triu_f, HIGHEST)
        cb = _dot_nt(cm[rows], bm[rows])
        h_all = h_ref[...]
        y = _dot_nt(cm[rows], h_all.astype(BF16)) * jnp.exp(acs)
        xdt_c = xdt[rows]
        intra = []
        for h in range(hg):
            col = acs[:, h * p:h * p + 1]
            row = acs_t[h:h + 1, :]
            seg = jnp.where(tril, jnp.exp(jnp.minimum(col - row, 0.0)), 0.0)
            intra.append(_dot((cb * seg).astype(BF16), xdt_c[:, h * p:(h + 1) * p].astype(BF16)))
        y = y + jnp.concatenate(intra, axis=1)
        y = y + x[rows] * dskip
        y = y * zg[rows]
        o_ref[rows, :] = (_rms(y) * nw).astype(o_ref.dtype)
        acs_end = acs[chunk - 1:chunk, :]
        states = _dot_tn((xdt_c * jnp.exp(acs_end - acs)).astype(BF16), bm[rows])
        for h in range(hg):
            dec = jnp.exp(acs_t[h:h + 1, chunk - 1:chunk])
            hs = slice(h * p, (h + 1) * p)
            h_ref[hs, :] = h_all[hs] * dec + states[hs]


def ssd_branch(proj, conv, small, dt_t, dt_bias, a_log, d_skip, norm_w, *, batch, seq, tb=256, out_dtype=BF16):
    t = proj.shape[0]
    nb = seq // tb
    hg = M_HEADS // M_GROUPS
    gw = M_GROUP_WIDTH
    rep = lambda v: jnp.repeat(v.astype(F32), M_HEAD_DIM).reshape(1, M_WIDTH)
    a = -jnp.exp(a_log.astype(F32))
    expand = np.zeros((M_GROUPS, SMALL_WIDTH, gw), np.float32)
    for g in range(M_GROUPS):
        for h in range(hg):
            expand[g, S_DT + g * hg + h, h * M_HEAD_DIM:(h + 1) * M_HEAD_DIM] = 1.0
    dtb_pad = jnp.zeros((1, SMALL_WIDTH), F32).at[0, S_DT:S_DT + M_HEADS].set(dt_bias.astype(F32))
    row = lambda i_of: pl.BlockSpec((tb, gw), i_of)
    return pl.pallas_call(
        functools.partial(_ssd_kernel, chunk=M_CHUNK),
        grid=(batch, M_GROUPS, nb),
        in_specs=[pl.BlockSpec((tb, gw), lambda b, g, i: (b * nb + i, g)),
                  pl.BlockSpec((tb, gw), lambda b, g, i: (b * nb + i, g)),
                  pl.BlockSpec((tb, M_STATE), lambda b, g, i: (b * nb + i, M_WIDTH // M_STATE + g)),
                  pl.BlockSpec((tb, M_STATE), lambda b, g, i: (b * nb + i, (M_WIDTH + M_GROUPS * M_STATE) // M_STATE + g)),
                  pl.BlockSpec((tb, SMALL_WIDTH), lambda b, g, i: (b * nb + i, 0)),
                  pl.BlockSpec((None, hg, tb), lambda b, g, i: (g, 0, b * nb + i)),
                  pl.BlockSpec((1, SMALL_WIDTH), lambda b, g, i: (0, 0)),
                  pl.BlockSpec((None, hg, 1), lambda b, g, i: (g, 0, 0)),
                  pl.BlockSpec((1, gw), lambda b, g, i: (0, g)),
                  pl.BlockSpec((None, hg, 1), lambda b, g, i: (g, 0, 0)),
                  pl.BlockSpec((None, SMALL_WIDTH, gw), lambda b, g, i: (g, 0, 0)),
                  pl.BlockSpec((1, gw), lambda b, g, i: (0, g)),
                  pl.BlockSpec((1, gw), lambda b, g, i: (0, g))],
        out_specs=pl.BlockSpec((tb, gw), lambda b, g, i: (b * nb + i, g)),
        out_shape=jax.ShapeDtypeStruct((t, M_WIDTH), out_dtype),
        scratch_shapes=[pltpu.VMEM((gw, M_STATE), F32)],
        compiler_params=_cparams(("parallel", "parallel", "arbitrary")),
        name="ssd",
    )(proj, conv, conv, conv, small, dt_t, dtb_pad, dt_bias.astype(F32).reshape(M_GROUPS, hg, 1),
      rep(a), a.reshape(M_GROUPS, hg, 1), jnp.asarray(expand), rep(d_skip), norm_w.astype(F32).reshape(1, M_WIDTH))


def _gdn_kernel(q_ref, k_ref, v_ref, g_ref, small_ref, at_ref, sel_ref, alog_ref, dtb_ref, nw_ref,
                o_ref, s_ref, *, chunk):
    @pl.when(pl.program_id(2) == 0)
    def _():
        s_ref[...] = jnp.zeros_like(s_ref)

    tb = q_ref.shape[0]
    dk = DN_KEY
    tril = _tril(chunk)
    strict = _tril(chunk, strict=True)
    tril_f = tril.astype(F32)
    triu_f = (lax.broadcasted_iota(jnp.int32, (chunk, chunk), 0)
              <= lax.broadcasted_iota(jnp.int32, (chunk, chunk), 1)).astype(F32)
    eye = (lax.broadcasted_iota(jnp.int32, (chunk, chunk), 0)
           == lax.broadcasted_iota(jnp.int32, (chunk, chunk), 1)).astype(F32)
    ab = _dot(small_ref[...], sel_ref[...], HIGHEST)
    neg_a = -jnp.exp(alog_ref[...])
    la = neg_a * _softplus(ab[:, :dk] + dtb_ref[...])
    beta = _sigmoid(ab[:, dk:])
    la_t = neg_a[:, 0:1] * _softplus(at_ref[...] + dtb_ref[:, 0:1])
    q = q_ref[...]
    k = k_ref[...]
    q = q * lax.rsqrt(jnp.sum(q * q, axis=-1, keepdims=True) + 1e-6) * dk ** -0.5
    k = k * lax.rsqrt(jnp.sum(k * k, axis=-1, keepdims=True) + 1e-6)
    kb = k * beta
    vb = v_ref[...] * beta
    gate = _silu(g_ref[...])
    nw = nw_ref[...]
    s = s_ref[...]
    for c in range(tb // chunk):
        rows = slice(c * chunk, (c + 1) * chunk)
        dec = _dot(tril_f, la[rows], HIGHEST)
        dec_t = _dot(la_t[:, rows], triu_f, HIGHEST)
        lmask = jnp.where(tril, jnp.exp(jnp.minimum(dec[:, :chunk] - dec_t, 0.0)), 0.0)
        kc = k[rows]
        kcb = kc.astype(BF16)
        a_mat = jnp.where(strict, _dot_nt(kb[rows].astype(BF16), kcb) * lmask, 0.0)
        pw = -a_mat
        inv = eye + pw
        for _ in range(int(math.log2(chunk)) - 1):
            pw = _dot(pw, pw, HIGHEST)
            inv = inv + _dot(inv, pw, HIGHEST)
        e_dec = jnp.exp(dec)
        rhs = jnp.concatenate([kb[rows] * e_dec, vb[rows]], axis=1)
        sol = _dot(inv, rhs, HIGHEST)
        k_cum, u = sol[:, :dk], sol[:, dk:]
        qc = q[rows]
        att = _dot_nt(qc.astype(BF16), kcb) * lmask
        sb = s.astype(BF16)
        v_new = u - _dot(k_cum.astype(BF16), sb)
        o = _dot((qc * e_dec).astype(BF16), sb) + _dot(att.astype(BF16), v_new.astype(BF16))
        o_ref[rows, :] = (_rms(o) * nw * gate[rows]).astype(o_ref.dtype)
        dec_end = dec[chunk - 1:chunk, :]
        k_end = (kc * jnp.exp(dec_end - dec)).astype(BF16)
        s = s * jnp.exp(dec_t[:, chunk - 1:chunk]) + _dot_tn(k_end, v_new.astype(BF16))
    s_ref[...] = s


def gdn_branch(proj, conv, small, small_t, a_log, dt_bias, norm_w, *, batch, seq, tb=256, out_dtype=BF16):
    t = proj.shape[0]
    nb = seq // tb
    sel = np.zeros((DN_HEADS, SMALL_WIDTH, 2 * DN_KEY), np.float32)
    for h in range(DN_HEADS):
        sel[h, S_DA + h, :DN_KEY] = 1.0
        sel[h, S_DB + h, DN_KEY:] = 1.0
    rep = lambda v: jnp.broadcast_to(v.astype(F32).reshape(DN_HEADS, 1, 1), (DN_HEADS, 1, DN_KEY))
    hd = DN_HEADS

    def col(off):
        return pl.BlockSpec((tb, DN_KEY), lambda b, h, i: (b * nb + i, off + h))

    return pl.pallas_call(
        functools.partial(_gdn_kernel, chunk=DN_CHUNK),
        grid=(batch, DN_HEADS, nb),
        in_specs=[col(0), col(hd), col(2 * hd), col(C_DG // DN_VAL),
                  pl.BlockSpec((tb, SMALL_WIDTH), lambda b, h, i: (b * nb + i, 0)),
                  pl.BlockSpec((None, 1, tb), lambda b, h, i: (S_DA + h, 0, b * nb + i)),
                  pl.BlockSpec((None, SMALL_WIDTH, 2 * DN_KEY), lambda b, h, i: (h, 0, 0)),
                  pl.BlockSpec((None, 1, DN_KEY), lambda b, h, i: (h, 0, 0)),
                  pl.BlockSpec((None, 1, DN_KEY), lambda b, h, i: (h, 0, 0)),
                  pl.BlockSpec((1, DN_VAL), lambda b, h, i: (0, 0))],
        out_specs=pl.BlockSpec((tb, DN_VAL), lambda b, h, i: (b * nb + i, h)),
        out_shape=jax.ShapeDtypeStruct((t, DN_HEADS * DN_VAL), out_dtype),
        scratch_shapes=[pltpu.VMEM((DN_KEY, DN_VAL), F32)],
        compiler_params=_cparams(("parallel", "parallel", "arbitrary")),
        name="gdn",
    )(conv, conv, conv, proj, small, small_t, jnp.asarray(sel), rep(a_log), rep(dt_bias),
      norm_w.astype(F32).reshape(1, DN_VAL))


def _layer_norm(y, gain, bias):
    yc = y - jnp.mean(y, axis=-1, keepdims=True)
    return yc * lax.rsqrt(jnp.mean(yc * yc, axis=-1, keepdims=True) + NORM_EPS) * gain + bias


def _merge_kernel(x_ref, *refs):
    wm, bm, ys, wb, o_ref = refs[0:4], refs[4:8], refs[8:12], refs[12:16], refs[16]
    x = x_ref[...]
    acc = None
    for n in range(N_BRANCH):
        gate = _sigmoid(_dot(x, wm[n][...]) + bm[n][...])
        term = gate * _dot(ys[n][...], wb[n][...])
        acc = term if acc is None else acc + term
    o_ref[...] = acc.astype(o_ref.dtype)


def merge_branches(xb, w_merge, b_merge, ys, w_branch, *, tm=256, tn=512):
    t, d = xb.shape
    nj = d // tn
    wm_specs = [pl.BlockSpec((d, tn), functools.partial(lambda j, i, n: (0, n * nj + j), n=n)) for n in range(N_BRANCH)]
    bm_specs = [pl.BlockSpec((1, tn), functools.partial(lambda j, i, n: (0, n * nj + j), n=n)) for n in range(N_BRANCH)]
    y_specs = [pl.BlockSpec((tm, BRANCH_WIDTH), lambda j, i: (i, 0)) for _ in range(N_BRANCH)]
    wb_specs = [pl.BlockSpec((None, BRANCH_WIDTH, tn), functools.partial(lambda j, i, n: (n, 0, j), n=n)) for n in range(N_BRANCH)]
    return pl.pallas_call(
        _merge_kernel,
        grid=(nj, t // tm),
        in_specs=[pl.BlockSpec((tm, d), lambda j, i: (i, 0))] + wm_specs + bm_specs + y_specs + wb_specs,
        out_specs=pl.BlockSpec((tm, tn), lambda j, i: (i, j)),
        out_shape=jax.ShapeDtypeStruct((t, d), BF16),
        compiler_params=_cparams(("parallel", "parallel")),
        name="merge",
    )(xb, *([w_merge] * N_BRANCH), *([b_merge.reshape(1, -1)] * N_BRANCH), *ys, *([w_branch] * N_BRANCH))


def _out_ln_kernel(m_ref, w_ref, x_ref, g_ref, b_ref, o_ref, ob_ref):
    y = ALPHA * x_ref[...] + _dot(m_ref[...], w_ref[...])
    y = _layer_norm(y, g_ref[...], b_ref[...])
    o_ref[...] = y
    ob_ref[...] = y.astype(BF16)


def out_proj_ln(merged, w_out, x, gain, bias, *, tm=512):
    t, d = x.shape
    return pl.pallas_call(
        _out_ln_kernel,
        grid=(t // tm,),
        in_specs=[pl.BlockSpec((tm, d), lambda i: (i, 0)),
                  pl.BlockSpec((d, d), lambda i: (0, 0)),
                  pl.BlockSpec((tm, d), lambda i: (i, 0)),
                  pl.BlockSpec((1, d), lambda i: (0, 0)),
                  pl.BlockSpec((1, d), lambda i: (0, 0))],
        out_specs=[pl.BlockSpec((tm, d), lambda i: (i, 0)), pl.BlockSpec((tm, d), lambda i: (i, 0))],
        out_shape=[jax.ShapeDtypeStruct((t, d), F32), jax.ShapeDtypeStruct((t, d), BF16)],
        compiler_params=_cparams(("parallel",)),
        name="out_proj_ln",
    )(merged, w_out, x, gain.reshape(1, d), bias.reshape(1, d))


def _pair_candidates(k):
    return [(i, j) for i in range(k) for j in range(k) if (i + 1) * (j + 1) <= k]


def _top_values(s, k):
    out = []
    for r in range(k):
        m = jnp.max(s, axis=0, keepdims=True)
        out.append(m)
        if r + 1 < k:
            s = jnp.where(s == m, -jnp.inf, s)
    return out


def _peer_select_kernel(x_ref, wq_ref, keys_ref, ea_ref, eb_ref, tau_ref):
    q = _dot(x_ref[...], wq_ref[...])
    k1 = keys_ref[0].astype(BF16)
    k2 = keys_ref[1].astype(BF16)
    s1, s2, v1, v2 = [], [], [], []
    for h in range(P_HEADS):
        qa = q[:, (2 * h) * P_HALF:(2 * h + 1) * P_HALF].astype(BF16)
        qb = q[:, (2 * h + 1) * P_HALF:(2 * h + 2) * P_HALF].astype(BF16)
        s1.append(_dot_nt(k1, qa))
        s2.append(_dot_nt(k2, qb))
        v1.append(_top_values(s1[h], P_TOPK))
        v2.append(_top_values(s2[h], P_TOPK))
    r1 = [jnp.concatenate([v1[h][r] for h in range(P_HEADS)], axis=0) for r in range(P_TOPK)]
    r2 = [jnp.concatenate([v2[h][r] for h in range(P_HEADS)], axis=0) for r in range(P_TOPK)]
    pairs = _pair_candidates(P_TOPK)
    cand = [r1[i] + r2[j] for i, j in pairs]
    work = list(cand)
    thr = None
    for r in range(P_TOPK):
        thr = functools.reduce(jnp.maximum, work)
        if r + 1 < P_TOPK:
            work = [jnp.where(w == thr, -jnp.inf, w) for w in work]
    m1, m2 = r1[0], r2[0]
    z = functools.reduce(lambda a, b: a + b,
                         [jnp.where(c >= thr, jnp.exp(c - (m1 + m2)), 0.0) for c in cand])
    rz = 1.0 / z
    ea_top = [jnp.exp(r - m1) * rz for r in r1]
    eb_top = [jnp.exp(r - m2) for r in r2]
    tau = functools.reduce(jnp.minimum,
                           [jnp.where(c >= thr, ea_top[i] * eb_top[j], jnp.inf) for c, (i, j) in zip(cand, pairs)])
    tau_ref[...] = tau
    for h in range(P_HEADS):
        ea_ref[h] = jnp.exp(s1[h] - m1[h:h + 1, :]) * rz[h:h + 1, :]
        eb_ref[h] = jnp.exp(s2[h] - m2[h:h + 1, :])


def peer_select(xb, w_query, sub_keys, *, tn=256):
    t, d = xb.shape
    return pl.pallas_call(
        _peer_select_kernel,
        grid=(t // tn,),
        in_specs=[pl.BlockSpec((tn, d), lambda i: (i, 0)),
                  pl.BlockSpec((d, P_HEADS * P_QDIM), lambda i: (0, 0)),
                  pl.BlockSpec((2, P_NKEYS, P_HALF), lambda i: (0, 0, 0))],
        out_specs=[pl.BlockSpec((P_HEADS, P_NKEYS, tn), lambda i: (0, 0, i)),
                   pl.BlockSpec((P_HEADS, P_NKEYS, tn), lambda i: (0, 0, i)),
                   pl.BlockSpec((P_HEADS, tn), lambda i: (0, i))],
        out_shape=[jax.ShapeDtypeStruct((P_HEADS, P_NKEYS, t), F32),
                   jax.ShapeDtypeStruct((P_HEADS, P_NKEYS, t), F32),
                   jax.ShapeDtypeStruct((P_HEADS, t), F32)],
        compiler_params=_cparams(("parallel",)),
        name="peer_select",
    )(xb, w_query, sub_keys)


def _gelu_tanh(x):
    return 0.5 * x * (1.0 + jnp.tanh(math.sqrt(2.0 / math.pi) * (x + 0.044715 * (x * x * x))))


def _peer_ffn_kernel(xb_ref, u_ref, v_ref, ea_ref, eb_ref, tau_ref, x_ref, g_ref, b_ref,
                     o_ref, ob_ref, acc_ref, *, rows_per_step):
    e = pl.program_id(1)

    @pl.when(e == 0)
    def _():
        acc_ref[...] = jnp.zeros_like(acc_ref)

    act = _dot_nt(u_ref[...], xb_ref[...])
    gates = []
    for r in range(rows_per_step):
        a = e * rows_per_step + r
        g = None
        for h in range(P_HEADS):
            p = ea_ref[h, pl.ds(a, 1), :] * eb_ref[h]
            term = jnp.where(p >= tau_ref[h:h + 1, :], p, 0.0)
            g = term if g is None else g + term
        gates.append(g)
    hidden = (jnp.concatenate(gates, axis=0) * _gelu_tanh(act)).astype(BF16)
    acc_ref[...] += _dot_tn(hidden, v_ref[...])

    @pl.when(e == pl.num_programs(1) - 1)
    def _():
        y = _layer_norm(ALPHA * x_ref[...] + acc_ref[...], g_ref[...], b_ref[...])
        o_ref[...] = y
        ob_ref[...] = y.astype(BF16)


def peer_ffn_ln(xb, x, expert_u, expert_v, ea, eb, tau, gain, bias, *, tn=512, te=512):
    t, d = x.shape
    n_e = expert_u.shape[0]
    tok = lambda i, e: (i, 0)
    once = pl.Buffered(1)
    return pl.pallas_call(
        functools.partial(_peer_ffn_kernel, rows_per_step=te // P_NKEYS),
        grid=(t // tn, n_e // te),
        in_specs=[pl.BlockSpec((tn, d), tok, pipeline_mode=once),
                  pl.BlockSpec((te, d), lambda i, e: (e, 0)),
                  pl.BlockSpec((te, d), lambda i, e: (e, 0)),
                  pl.BlockSpec((P_HEADS, P_NKEYS, tn), lambda i, e: (0, 0, i), pipeline_mode=once),
                  pl.BlockSpec((P_HEADS, P_NKEYS, tn), lambda i, e: (0, 0, i), pipeline_mode=once),
                  pl.BlockSpec((P_HEADS, tn), lambda i, e: (0, i)),
                  pl.BlockSpec((tn, d), tok, pipeline_mode=once),
                  pl.BlockSpec((1, d), lambda i, e: (0, 0)),
                  pl.BlockSpec((1, d), lambda i, e: (0, 0))],
        out_specs=[pl.BlockSpec((tn, d), tok), pl.BlockSpec((tn, d), tok)],
        out_shape=[jax.ShapeDtypeStruct((t, d), F32), jax.ShapeDtypeStruct((t, d), BF16)],
        scratch_shapes=[pltpu.VMEM((tn, d), F32)],
        compiler_params=_cparams(("parallel", "arbitrary")),
        name="peer_ffn",
    )(xb, expert_u, expert_v, ea, eb, tau, x, gain.reshape(1, d), bias.reshape(1, d))


_ORIG_SIZES = (M_WIDTH, M_XBC, M_HEADS,
               H_HEADS * H_KEY, H_HEADS * H_KEY, H_HEADS * H_VAL, H_HEADS * H_VAL,
               DN_HEADS * DN_KEY, DN_HEADS * DN_KEY, DN_HEADS * DN_VAL, DN_HEADS, DN_HEADS, DN_HEADS * DN_VAL,
               G_HEADS * G_KEY, G_HEADS * G_KEY, G_HEADS * G_VAL, G_HEADS * G_VAL, G_GATE_RANK)
_NARROW = (2, 10, 11, 17)


def _split_w_in(w_in):
    offs = np.concatenate([[0], np.cumsum(_ORIG_SIZES)])
    wide = [w_in[:, :, offs[i]:offs[i + 1]] for i in range(len(_ORIG_SIZES)) if i not in _NARROW]
    narrow = [w_in[:, :, offs[i]:offs[i + 1]] for i in _NARROW]
    n_narrow = sum(_ORIG_SIZES[i] for i in _NARROW)
    pad = jnp.zeros(w_in.shape[:2] + (SMALL_WIDTH - n_narrow,), w_in.dtype)
    return jnp.concatenate(wide, axis=-1).astype(BF16), jnp.concatenate(narrow + [pad], axis=-1).astype(F32)


def kernel(x, w_in, m_conv_w, m_conv_b, m_dt_bias, m_a_log, m_d_skip, m_norm_w, h_lb_logits, h_norm_w, dn_conv_w, dn_a_log, dn_dt_bias, dn_norm_w, g_gate_w, g_gate_b, g_norm_w, w_branch, w_merge, b_merge, w_out, ln1_w, ln1_b, p_w_query, p_sub_keys, p_expert_u, p_expert_v, ln2_w, ln2_b):
    bsz, seq, d = x.shape
    t = bsz * seq
    w_big, w_small = _split_w_in(w_in)
    lb_p = jax.nn.softmax(h_lb_logits.astype(F32), axis=0)
    lower_bounds = jnp.cumsum(lb_p, axis=0) - lb_p[0]
    gate_w_pad = jnp.zeros((DEPTH, SMALL_WIDTH, G_HEADS * G_KEY), F32).at[:, S_LR:S_LR + G_GATE_RANK].set(g_gate_w.astype(F32))
    zero_bias = jnp.zeros((3 * DN_HEADS * DN_KEY,), F32)
    w_merge_b, w_branch_b, w_out_b = w_merge.astype(BF16), w_branch.astype(BF16), w_out.astype(BF16)
    wq_b, u_b, v_b = p_w_query.astype(BF16), p_expert_u.astype(BF16), p_expert_v.astype(BF16)

    xf = x.reshape(t, d).astype(F32)
    xb = xf.astype(BF16)
    for l in range(DEPTH):
        proj = matmul(xb, w_big[l], tm=512, tn=1024)
        small = matmul(xf, w_small[l], tm=512, tn=SMALL_WIDTH, precision=HIGHEST)
        small_t = small.T
        conv_m = conv_silu(proj, m_conv_w[l].astype(F32), m_conv_b[l].astype(F32), col0=C_XBC, seq=seq)
        conv_d = conv_silu(proj, dn_conv_w[l].astype(F32), zero_bias, col0=C_DQ, seq=seq)
        dt_t = small_t[S_DT:S_DT + M_HEADS].reshape(M_GROUPS, M_HEADS // M_GROUPS, t)
        y_m = ssd_branch(proj, conv_m, small, dt_t, m_dt_bias[l], m_a_log[l], m_d_skip[l], m_norm_w[l],
                         batch=bsz, seq=seq)
        y_h = hgrn2_branch(proj, lower_bounds[l], h_norm_w[l].astype(F32), batch=bsz, seq=seq)
        y_d = gdn_branch(proj, conv_d, small, small_t.reshape(SMALL_WIDTH, 1, t), dn_a_log[l], dn_dt_bias[l],
                         dn_norm_w[l], batch=bsz, seq=seq)
        y_g = gla_branch(proj, small, gate_w_pad[l], g_gate_b[l].astype(F32), g_norm_w[l].astype(F32),
                         batch=bsz, seq=seq)
        merged = merge_branches(xb, w_merge_b[l], b_merge[l].astype(F32), (y_m, y_h, y_d, y_g), w_branch_b[l])
        xf, xb = out_proj_ln(merged, w_out_b[l], xf, ln1_w[l].astype(F32), ln1_b[l].astype(F32))
        ea, eb, tau = peer_select(xb, wq_b[l], p_sub_keys[l].astype(F32))
        xf, xb = peer_ffn_ln(xb, xf, u_b[l], v_b[l], ea, eb, tau, ln2_w[l].astype(F32), ln2_b[l].astype(F32))
    return xf.reshape(bsz, seq, d).astype(x.dtype)
```

```python
import functools
import math

import numpy as np
import jax
import jax.numpy as jnp
from jax import lax
from jax.experimental import pallas as pl
from jax.experimental.pallas import tpu as pltpu

F32 = jnp.float32
BF16 = jnp.bfloat16
HIGHEST = lax.Precision.HIGHEST
NORM_EPS = 1e-5

D_MODEL = 2048
DEPTH = 4

M_HEADS, M_HEAD_DIM, M_GROUPS, M_STATE, M_CONV, M_CHUNK = 16, 64, 4, 128, 4, 64
M_WIDTH = M_HEADS * M_HEAD_DIM
M_XBC = M_WIDTH + 2 * M_GROUPS * M_STATE
M_GROUP_WIDTH = M_WIDTH // M_GROUPS

H_HEADS, H_KEY, H_VAL, H_CHUNK = 8, 128, 128, 16
DN_HEADS, DN_KEY, DN_VAL, DN_CONV, DN_CHUNK = 8, 128, 128, 4, 64
G_HEADS, G_KEY, G_VAL, G_GATE_RANK, G_GATE_TEMP, G_CHUNK = 4, 128, 256, 16, 16.0, 64
N_BRANCH, BRANCH_WIDTH = 4, 1024

P_HEADS, P_NKEYS, P_QDIM, P_TOPK = 8, 128, 256, 16
P_HALF = P_QDIM // 2
P_EXPERTS = P_NKEYS * P_NKEYS

ALPHA = (2 * DEPTH) ** 0.25

C_MZ, C_XBC = 0, 1024
C_HQ, C_HF, C_HI, C_HG = 3072, 4096, 5120, 6144
C_DQ, C_DG = 7168, 10240
C_GQ, C_GK, C_GV, C_GG = 11264, 11776, 12288, 13312
BIG_WIDTH = 14336
S_DT, S_DA, S_DB, S_LR = 0, 16, 24, 32
SMALL_WIDTH = 128
LANES = 128

VMEM_LIMIT = 48 * 1024 * 1024
PEER_VMEM_LIMIT = 56 * 1024 * 1024


def _cparams(sem, limit=None):
    return pltpu.CompilerParams(dimension_semantics=sem, vmem_limit_bytes=limit or VMEM_LIMIT)


def _dot(a, b, precision=None):
    return jnp.dot(a, b, preferred_element_type=F32, precision=precision)


def _dot_nt(a, b, precision=None):
    return lax.dot_general(a, b, (((1,), (1,)), ((), ())), preferred_element_type=F32, precision=precision)


def _dot_tn(a, b, precision=None):
    return lax.dot_general(a, b, (((0,), (0,)), ((), ())), preferred_element_type=F32, precision=precision)


def _sigmoid(x):
    return 1.0 / (1.0 + jnp.exp(-x))


def _silu(x):
    return x * _sigmoid(x)


def _softplus(x):
    return jnp.maximum(x, 0.0) + jnp.log(1.0 + jnp.exp(-jnp.abs(x)))


def _tril(n, strict=False):
    r = lax.broadcasted_iota(jnp.int32, (n, n), 0)
    c = lax.broadcasted_iota(jnp.int32, (n, n), 1)
    return (r > c) if strict else (r >= c)


def _matmul_kernel(x_ref, w_ref, o_ref, *, precision):
    o_ref[...] = _dot(x_ref[...], w_ref[...], precision).astype(o_ref.dtype)


def matmul(x, w, *, tm, tn, out_dtype=F32, precision=None):
    m, k = x.shape
    n = w.shape[1]
    return pl.pallas_call(
        functools.partial(_matmul_kernel, precision=precision),
        grid=(n // tn, m // tm),
        in_specs=[pl.BlockSpec((tm, k), lambda j, i: (i, 0)),
                  pl.BlockSpec((k, tn), lambda j, i: (0, j))],
        out_specs=pl.BlockSpec((tm, tn), lambda j, i: (i, j)),
        out_shape=jax.ShapeDtypeStruct((m, n), out_dtype),
        compiler_params=_cparams(("parallel", "parallel")),
        name="matmul",
    )(x, w)


def _conv_kernel(x_ref, halo_ref, w_ref, b_ref, o_ref, *, blocks_per_seq, taps):
    i = pl.program_id(0)
    first = (i % blocks_per_seq) == 0
    tb = x_ref.shape[0]
    halo = jnp.where(first, 0.0, halo_ref[...])
    ext = jnp.concatenate([halo, x_ref[...]], axis=0)
    acc = b_ref[...]
    for j in range(taps):
        off = 8 - (taps - 1) + j
        acc = acc + w_ref[j:j + 1, :] * ext[off:off + tb, :]
    o_ref[...] = _silu(acc).astype(o_ref.dtype)


def conv_silu(proj, w, b, *, col0, seq, tb=256, tc=512, out_dtype=F32):
    t = proj.shape[0]
    taps, width = w.shape
    cb0 = col0 // tc
    hb = tb // 8
    return pl.pallas_call(
        functools.partial(_conv_kernel, blocks_per_seq=seq // tb, taps=taps),
        grid=(t // tb, width // tc),
        in_specs=[pl.BlockSpec((tb, tc), lambda i, j: (i, cb0 + j)),
                  pl.BlockSpec((8, tc), lambda i, j: (jnp.maximum(i * hb - 1, 0), cb0 + j)),
                  pl.BlockSpec((taps, tc), lambda i, j: (0, j)),
                  pl.BlockSpec((1, tc), lambda i, j: (0, j))],
        out_specs=pl.BlockSpec((tb, tc), lambda i, j: (i, j)),
        out_shape=jax.ShapeDtypeStruct((t, width), out_dtype),
        compiler_params=_cparams(("parallel", "parallel")),
        name="conv_silu",
    )(proj, proj, w, b.reshape(1, width))


def _gla_chunks(q, k, v, log_g, st_ref, chunk, emit):
    tb = q.shape[0]
    shift = int(math.log2(chunk))
    r = lax.broadcasted_iota(jnp.int32, (tb, tb), 0)
    c = lax.broadcasted_iota(jnp.int32, (tb, tb), 1)
    incl = (lax.shift_right_logical(r, shift) == lax.shift_right_logical(c, shift)) & (r >= c)
    base = lax.shift_left(lax.shift_right_logical(r, shift), shift)
    tril_b = jnp.where(incl, 1.0, 0.0).astype(BF16)
    sel_mid = jnp.where(c == base + (chunk // 2 - 1), 1.0, 0.0).astype(BF16)
    sel_end = jnp.where(c == base + (chunk - 1), 1.0, 0.0).astype(BF16)
    b = _sel_dot(tril_b, log_g)
    b_mid = _sel_dot(sel_mid, b)
    b_end = _sel_dot(sel_end, b)
    qf = (q * jnp.exp(b - b_mid)).astype(BF16)
    kf = (k * jnp.exp(b_mid - b)).astype(BF16)
    att = jnp.where(incl, _dot_nt(qf, kf), 0.0).astype(BF16)
    vb = v.astype(BF16)
    o_intra = _dot(att, vb)
    qd = (q * jnp.exp(b)).astype(BF16)
    k_end = (k * jnp.exp(b_end - b)).astype(BF16)
    g_end = jnp.exp(b_end)
    st = st_ref[...]
    for ci in range(tb // chunk):
        rows = slice(ci * chunk, (ci + 1) * chunk)
        emit(ci, o_intra[rows] + _dot_nt(qd[rows], st.astype(BF16)))
        st = st * g_end[ci * chunk:ci * chunk + 1, :] + _dot_tn(vb[rows], k_end[rows])
    st_ref[...] = st


def _rms(x):
    return x * lax.rsqrt(jnp.mean(x * x, axis=-1, keepdims=True) + NORM_EPS)


def _hgrn2_kernel(q_ref, f_ref, i_ref, g_ref, lb_ref, nw_ref, o_ref, st_ref, *, chunk):
    @pl.when(pl.program_id(2) == 0)
    def _():
        st_ref[...] = jnp.zeros_like(st_ref)

    lb = lb_ref[...]
    pre = f_ref[...]
    forget = lb + (1.0 - lb) * _sigmoid(pre)
    k = (1.0 - lb) * _sigmoid(-pre)
    gate = _sigmoid(g_ref[...])
    nw = nw_ref[...]

    def emit(c, o):
        rows = slice(c * chunk, (c + 1) * chunk)
        o_ref[rows, :] = (_rms(o) * nw * gate[rows]).astype(o_ref.dtype)

    _gla_chunks(_silu(q_ref[...]), k, i_ref[...], jnp.log(forget), st_ref, chunk, emit)


def hgrn2_branch(proj, lower_bound, norm_w, *, batch, seq, tb=256, out_dtype=BF16):
    t = proj.shape[0]
    nb = seq // tb
    kb = H_KEY // LANES

    def col(c0):
        return pl.BlockSpec((tb, H_KEY), lambda b, h, i: (b * nb + i, c0 // H_KEY + h))

    return pl.pallas_call(
        functools.partial(_hgrn2_kernel, chunk=H_CHUNK),
        grid=(batch, H_HEADS, nb),
        in_specs=[col(C_HQ), col(C_HF), col(C_HI), col(C_HG),
                  pl.BlockSpec((1, H_KEY), lambda b, h, i: (0, h)),
                  pl.BlockSpec((1, H_VAL), lambda b, h, i: (0, 0))],
        out_specs=pl.BlockSpec((tb, H_VAL), lambda b, h, i: (b * nb + i, h)),
        out_shape=jax.ShapeDtypeStruct((t, H_HEADS * H_VAL), out_dtype),
        scratch_shapes=[pltpu.VMEM((H_VAL, H_KEY), F32)],
        compiler_params=_cparams(("parallel", "parallel", "arbitrary")),
        name="hgrn2",
    )(proj, proj, proj, proj, lower_bound.reshape(1, -1), norm_w.reshape(1, -1))


def _gla_kernel(q_ref, k_ref, v_ref, g_ref, lr_ref, gw_ref, gb_ref, nw_ref, o_ref, st_ref, *, chunk):
    @pl.when(pl.program_id(2) == 0)
    def _():
        st_ref[...] = jnp.zeros_like(st_ref)

    z = _dot(lr_ref[...], gw_ref[...], HIGHEST) + gb_ref[...]
    log_a = (jnp.minimum(z, 0.0) - jnp.log(1.0 + jnp.exp(-jnp.abs(z)))) * (1.0 / G_GATE_TEMP)
    gate = _silu(g_ref[...])
    nw = nw_ref[...]

    def emit(c, o):
        rows = slice(c * chunk, (c + 1) * chunk)
        o_ref[rows, :] = (_rms(o) * nw * gate[rows]).astype(o_ref.dtype)

    _gla_chunks(q_ref[...] * G_KEY ** -0.5, k_ref[...], v_ref[...], log_a, st_ref, chunk, emit)


def gla_branch(proj, small, gate_w_pad, gate_b, norm_w, *, batch, seq, tb=256, out_dtype=BF16):
    t = proj.shape[0]
    nb = seq // tb

    def col(c0, w):
        return pl.BlockSpec((tb, w), lambda b, h, i: (b * nb + i, c0 // w + h))

    return pl.pallas_call(
        functools.partial(_gla_kernel, chunk=G_CHUNK),
        grid=(batch, G_HEADS, nb),
        in_specs=[col(C_GQ, G_KEY), col(C_GK, G_KEY), col(C_GV, G_VAL), col(C_GG, G_VAL),
                  pl.BlockSpec((tb, SMALL_WIDTH), lambda b, h, i: (b * nb + i, 0)),
                  pl.BlockSpec((SMALL_WIDTH, G_KEY), lambda b, h, i: (0, h)),
                  pl.BlockSpec((1, G_KEY), lambda b, h, i: (0, h)),
                  pl.BlockSpec((1, G_VAL), lambda b, h, i: (0, 0))],
        out_specs=pl.BlockSpec((tb, G_VAL), lambda b, h, i: (b * nb + i, h)),
        out_shape=jax.ShapeDtypeStruct((t, G_HEADS * G_VAL), out_dtype),
        scratch_shapes=[pltpu.VMEM((G_VAL, G_KEY), F32)],
        compiler_params=_cparams(("parallel", "parallel", "arbitrary")),
        name="gla",
    )(proj, proj, proj, proj, small, gate_w_pad, gate_b.reshape(1, -1), norm_w.reshape(1, -1))


def _ssd_kernel(z_ref, x_ref, b_ref, c_ref, small_ref, dtt_ref, dtb_ref, dtbc_ref, a_ref, ac_ref,
                e_ref, dskip_ref, nw_ref, o_ref, h_ref, *, chunk):
    @pl.when(pl.program_id(2) == 0)
    def _():
        h_ref[...] = jnp.zeros_like(h_ref)

    tb = x_ref.shape[0]
    hg = M_HEADS // M_GROUPS
    p = M_HEAD_DIM
    tril = _tril(chunk)
    tril_f = tril.astype(F32)
    triu_f = (lax.broadcasted_iota(jnp.int32, (chunk, chunk), 0)
              <= lax.broadcasted_iota(jnp.int32, (chunk, chunk), 1)).astype(F32)
    dt = _dot(_softplus(small_ref[...] + dtb_ref[...]), e_ref[...], HIGHEST)
    da = dt * a_ref[...]
    da_t = _softplus(dtt_ref[...] + dtbc_ref[...]) * ac_ref[...]
    x = x_ref[...]
    xdt = x * dt
    zg = _silu(z_ref[...])
    bm = b_ref[...].astype(BF16)
    cm = c_ref[...].astype(BF16)
    dskip = dskip_ref[...]
    nw = nw_ref[...]
    for c in range(tb // chunk):
        rows = slice(c * chunk, (c + 1) * chunk)
        acs = _dot(tril_f, da[rows], HIGHEST)
        acs_t = _dot(da_t[:, rows], triu_f, HIGHEST)
        cb = _dot_nt(cm[rows], bm[rows])
        h_all = h_ref[...]
        y = _dot_nt(cm[rows], h_all.astype(BF16)) * jnp.exp(acs)
        xdt_c = xdt[rows]
        intra = []
        for h in range(hg):
            col = acs[:, h * p:h * p + 1]
            row = acs_t[h:h + 1, :]
            seg = jnp.where(tril, jnp.exp(jnp.minimum(col - row, 0.0)), 0.0)
            intra.append(_dot((cb * seg).astype(BF16), xdt_c[:, h * p:(h + 1) * p].astype(BF16)))
        y = y + jnp.concatenate(intra, axis=1)
        y = y + x[rows] * dskip
        y = y * zg[rows]
        o_ref[rows, :] = (_rms(y) * nw).astype(o_ref.dtype)
        acs_end = acs[chunk - 1:chunk, :]
        states = _dot_tn((xdt_c * jnp.exp(acs_end - acs)).astype(BF16), bm[rows])
        for h in range(hg):
            dec = jnp.exp(acs_t[h:h + 1, chunk - 1:chunk])
            hs = slice(h * p, (h + 1) * p)
            h_ref[hs, :] = h_all[hs] * dec + states[hs]


def ssd_branch(proj, conv, small, dt_t, dt_bias, a_log, d_skip, norm_w, *, batch, seq, tb=256, out_dtype=BF16):
    t = proj.shape[0]
    nb = seq // tb
    hg = M_HEADS // M_GROUPS
    gw = M_GROUP_WIDTH
    rep = lambda v: jnp.repeat(v.astype(F32), M_HEAD_DIM).reshape(1, M_WIDTH)
    a = -jnp.exp(a_log.astype(F32))
    expand = np.zeros((M_GROUPS, SMALL_WIDTH, gw), np.float32)
    for g in range(M_GROUPS):
        for h in range(hg):
            expand[g, S_DT + g * hg + h, h * M_HEAD_DIM:(h + 1) * M_HEAD_DIM] = 1.0
    dtb_pad = jnp.zeros((1, SMALL_WIDTH), F32).at[0, S_DT:S_DT + M_HEADS].set(dt_bias.astype(F32))
    row = lambda i_of: pl.BlockSpec((tb, gw), i_of)
    return pl.pallas_call(
        functools.partial(_ssd_kernel, chunk=M_CHUNK),
        grid=(batch, M_GROUPS, nb),
        in_specs=[pl.BlockSpec((tb, gw), lambda b, g, i: (b * nb + i, g)),
                  pl.BlockSpec((tb, gw), lambda b, g, i: (b * nb + i, g)),
                  pl.BlockSpec((tb, M_STATE), lambda b, g, i: (b * nb + i, M_WIDTH // M_STATE + g)),
                  pl.BlockSpec((tb, M_STATE), lambda b, g, i: (b * nb + i, (M_WIDTH + M_GROUPS * M_STATE) // M_STATE + g)),
                  pl.BlockSpec((tb, SMALL_WIDTH), lambda b, g, i: (b * nb + i, 0)),
                  pl.BlockSpec((None, hg, tb), lambda b, g, i: (g, 0, b * nb + i)),
                  pl.BlockSpec((1, SMALL_WIDTH), lambda b, g, i: (0, 0)),
                  pl.BlockSpec((None, hg, 1), lambda b, g, i: (g, 0, 0)),
                  pl.BlockSpec((1, gw), lambda b, g, i: (0, g)),
                  pl.BlockSpec((None, hg, 1), lambda b, g, i: (g, 0, 0)),
                  pl.BlockSpec((None, SMALL_WIDTH, gw), lambda b, g, i: (g, 0, 0)),
                  pl.BlockSpec((1, gw), lambda b, g, i: (0, g)),
                  pl.BlockSpec((1, gw), lambda b, g, i: (0, g))],
        out_specs=pl.BlockSpec((tb, gw), lambda b, g, i: (b * nb + i, g)),
        out_shape=jax.ShapeDtypeStruct((t, M_WIDTH), out_dtype),
        scratch_shapes=[pltpu.VMEM((gw, M_STATE), F32)],
        compiler_params=_cparams(("parallel", "parallel", "arbitrary")),
        name="ssd",
    )(proj, conv, conv, conv, small, dt_t, dtb_pad, dt_bias.astype(F32).reshape(M_GROUPS, hg, 1),
      rep(a), a.reshape(M_GROUPS, hg, 1), jnp.asarray(expand), rep(d_skip), norm_w.astype(F32).reshape(1, M_WIDTH))


def _split3(x):
    hi = x.astype(BF16)
    r = x - hi.astype(F32)
    mid = r.astype(BF16)
    lo = (r - mid.astype(F32)).astype(BF16)
    return hi, mid, lo


def _dot_sel(x, sel):
    hi, mid, lo = _split3(x)
    return _dot(hi, sel) + _dot(mid, sel) + _dot(lo, sel)


def _sel_dot(sel, x):
    hi, mid, lo = _split3(x)
    return _dot(sel, hi) + _dot(sel, mid) + _dot(sel, lo)


def _gdn_kernel(q_ref, k_ref, v_ref, g_ref, small_ref, at_ref, sel_ref, alog_ref, dtb_ref, nw_ref,
                o_ref, s_ref, *, chunk):
    @pl.when(pl.program_id(2) == 0)
    def _():
        s_ref[...] = jnp.zeros_like(s_ref)

    tb = q_ref.shape[0]
    dk = DN_KEY
    shift = int(math.log2(chunk))
    r = lax.broadcasted_iota(jnp.int32, (tb, tb), 0)
    c = lax.broadcasted_iota(jnp.int32, (tb, tb), 1)
    same = lax.shift_right_logical(r, shift) == lax.shift_right_logical(c, shift)
    incl = same & (r >= c)
    strict = same & (r > c)
    tril_b = jnp.where(incl, 1.0, 0.0).astype(BF16)
    triu_b = jnp.where(same & (r <= c), 1.0, 0.0).astype(BF16)

    ab = _dot_sel(small_ref[...], sel_ref[...])
    neg_a = -jnp.exp(alog_ref[...])
    la = neg_a * _softplus(ab[:, :dk] + dtb_ref[...])
    beta = _sigmoid(ab[:, dk:])
    la_t = neg_a[:, 0:1] * _softplus(at_ref[...] + dtb_ref[:, 0:1])
    dec = _sel_dot(tril_b, la)
    dec_t = _dot_sel(jnp.broadcast_to(la_t, (8, tb)), triu_b)[0:1, :]
    dec_wide = jnp.concatenate([dec] * (tb // dk), axis=1)
    lmask = jnp.where(incl, jnp.exp(jnp.minimum(dec_wide - dec_t, 0.0)), 0.0)

    q = q_ref[...]
    k = k_ref[...]
    q = q * lax.rsqrt(jnp.sum(q * q, axis=-1, keepdims=True) + 1e-6) * dk ** -0.5
    k = k * lax.rsqrt(jnp.sum(k * k, axis=-1, keepdims=True) + 1e-6)
    kb = k * beta
    kbf = k.astype(BF16)
    e_dec = jnp.exp(dec)
    pw = jnp.where(strict, -(_dot_nt(kb.astype(BF16), kbf) * lmask), 0.0)
    y = jnp.concatenate([kb * e_dec, v_ref[...] * beta], axis=1)
    for n in range(shift):
        if n > 0:
            pwb = pw.astype(BF16)
            pw = _dot(pwb, pwb)
        y = y + _dot(pw.astype(BF16), y.astype(BF16))
    yb = y.astype(BF16)
    att = (_dot_nt(q.astype(BF16), kbf) * lmask).astype(BF16)
    av = _dot(att, yb)
    qp = (q * e_dec - av[:, :dk]).astype(BF16)
    o_intra = av[:, dk:]
    gate = _silu(g_ref[...])
    nw = nw_ref[...]
    s = s_ref[...]
    for ci in range(tb // chunk):
        rows = slice(ci * chunk, (ci + 1) * chunk)
        last = (ci + 1) * chunk - 1
        k_end = (k[rows] * jnp.exp(dec[last:last + 1, :] - dec[rows])).astype(BF16)
        wn = _dot_tn(k_end, yb[rows])
        sb = s.astype(BF16)
        o = _dot(qp[rows], sb) + o_intra[rows]
        o_ref[rows, :] = (_rms(o) * nw * gate[rows]).astype(o_ref.dtype)
        s = s * jnp.exp(dec_t[:, last:last + 1]) - _dot(wn[:, :dk].astype(BF16), sb) + wn[:, dk:]
    s_ref[...] = s


def gdn_branch(proj, conv, small, small_t, a_log, dt_bias, norm_w, *, batch, seq, tb=256, out_dtype=BF16):
    t = proj.shape[0]
    nb = seq // tb
    sel = np.zeros((DN_HEADS, SMALL_WIDTH, 2 * DN_KEY), np.float32)
    for h in range(DN_HEADS):
        sel[h, S_DA + h, :DN_KEY] = 1.0
        sel[h, S_DB + h, DN_KEY:] = 1.0
    rep = lambda v: jnp.broadcast_to(v.astype(F32).reshape(DN_HEADS, 1, 1), (DN_HEADS, 1, DN_KEY))
    hd = DN_HEADS

    def col(off):
        return pl.BlockSpec((tb, DN_KEY), lambda b, h, i: (b * nb + i, off + h))

    return pl.pallas_call(
        functools.partial(_gdn_kernel, chunk=DN_CHUNK),
        grid=(batch, DN_HEADS, nb),
        in_specs=[col(0), col(hd), col(2 * hd), col(C_DG // DN_VAL),
                  pl.BlockSpec((tb, SMALL_WIDTH), lambda b, h, i: (b * nb + i, 0)),
                  pl.BlockSpec((None, 1, tb), lambda b, h, i: (S_DA + h, 0, b * nb + i)),
                  pl.BlockSpec((None, SMALL_WIDTH, 2 * DN_KEY), lambda b, h, i: (h, 0, 0)),
                  pl.BlockSpec((None, 1, DN_KEY), lambda b, h, i: (h, 0, 0)),
                  pl.BlockSpec((None, 1, DN_KEY), lambda b, h, i: (h, 0, 0)),
                  pl.BlockSpec((1, DN_VAL), lambda b, h, i: (0, 0))],
        out_specs=pl.BlockSpec((tb, DN_VAL), lambda b, h, i: (b * nb + i, h)),
        out_shape=jax.ShapeDtypeStruct((t, DN_HEADS * DN_VAL), out_dtype),
        scratch_shapes=[pltpu.VMEM((DN_KEY, DN_VAL), F32)],
        compiler_params=_cparams(("parallel", "parallel", "arbitrary")),
        name="gdn",
    )(conv, conv, conv, proj, small, small_t, jnp.asarray(sel, BF16), rep(a_log), rep(dt_bias),
      norm_w.astype(F32).reshape(1, DN_VAL))


def _layer_norm(y, gain, bias):
    yc = y - jnp.mean(y, axis=-1, keepdims=True)
    return yc * lax.rsqrt(jnp.mean(yc * yc, axis=-1, keepdims=True) + NORM_EPS) * gain + bias


def _merge_kernel(x_ref, *refs):
    wm, bm, ys, wb, o_ref = refs[0:4], refs[4:8], refs[8:12], refs[12:16], refs[16]
    x = x_ref[...]
    acc = None
    for n in range(N_BRANCH):
        gate = _sigmoid(_dot(x, wm[n][...]) + bm[n][...])
        term = gate * _dot(ys[n][...], wb[n][...])
        acc = term if acc is None else acc + term
    o_ref[...] = acc.astype(o_ref.dtype)


def merge_branches(xb, w_merge, b_merge, ys, w_branch, *, tm=256, tn=512):
    t, d = xb.shape
    nj = d // tn
    wm_specs = [pl.BlockSpec((d, tn), functools.partial(lambda j, i, n: (0, n * nj + j), n=n)) for n in range(N_BRANCH)]
    bm_specs = [pl.BlockSpec((1, tn), functools.partial(lambda j, i, n: (0, n * nj + j), n=n)) for n in range(N_BRANCH)]
    y_specs = [pl.BlockSpec((tm, BRANCH_WIDTH), lambda j, i: (i, 0)) for _ in range(N_BRANCH)]
    wb_specs = [pl.BlockSpec((None, BRANCH_WIDTH, tn), functools.partial(lambda j, i, n: (n, 0, j), n=n)) for n in range(N_BRANCH)]
    return pl.pallas_call(
        _merge_kernel,
        grid=(nj, t // tm),
        in_specs=[pl.BlockSpec((tm, d), lambda j, i: (i, 0))] + wm_specs + bm_specs + y_specs + wb_specs,
        out_specs=pl.BlockSpec((tm, tn), lambda j, i: (i, j)),
        out_shape=jax.ShapeDtypeStruct((t, d), BF16),
        compiler_params=_cparams(("parallel", "parallel")),
        name="merge",
    )(xb, *([w_merge] * N_BRANCH), *([b_merge.reshape(1, -1)] * N_BRANCH), *ys, *([w_branch] * N_BRANCH))


def _out_ln_kernel(m_ref, w_ref, x_ref, g_ref, b_ref, o_ref, ob_ref):
    y = ALPHA * x_ref[...] + _dot(m_ref[...], w_ref[...])
    y = _layer_norm(y, g_ref[...], b_ref[...])
    o_ref[...] = y
    ob_ref[...] = y.astype(BF16)


def out_proj_ln(merged, w_out, x, gain, bias, *, tm=512):
    t, d = x.shape
    return pl.pallas_call(
        _out_ln_kernel,
        grid=(t // tm,),
        in_specs=[pl.BlockSpec((tm, d), lambda i: (i, 0)),
                  pl.BlockSpec((d, d), lambda i: (0, 0)),
                  pl.BlockSpec((tm, d), lambda i: (i, 0)),
                  pl.BlockSpec((1, d), lambda i: (0, 0)),
                  pl.BlockSpec((1, d), lambda i: (0, 0))],
        out_specs=[pl.BlockSpec((tm, d), lambda i: (i, 0)), pl.BlockSpec((tm, d), lambda i: (i, 0))],
        out_shape=[jax.ShapeDtypeStruct((t, d), F32), jax.ShapeDtypeStruct((t, d), BF16)],
        compiler_params=_cparams(("parallel",)),
        name="out_proj_ln",
    )(merged, w_out, x, gain.reshape(1, d), bias.reshape(1, d))


def _pair_candidates(k):
    return [(i, j) for i in range(k) for j in range(k) if (i + 1) * (j + 1) <= k]


def _top_values(s, k):
    out = []
    for r in range(k):
        m = jnp.max(s, axis=0, keepdims=True)
        out.append(m)
        if r + 1 < k:
            s = jnp.where(s == m, -jnp.inf, s)
    return out


def _peer_select_kernel(x_ref, wq_ref, keys_ref, ea_ref, eb_ref, tau_ref):
    q = _dot(x_ref[...], wq_ref[...])
    k1 = keys_ref[0].astype(BF16)
    k2 = keys_ref[1].astype(BF16)
    s1, s2, v1, v2 = [], [], [], []
    for h in range(P_HEADS):
        qa = q[:, (2 * h) * P_HALF:(2 * h + 1) * P_HALF].astype(BF16)
        qb = q[:, (2 * h + 1) * P_HALF:(2 * h + 2) * P_HALF].astype(BF16)
        s1.append(_dot_nt(k1, qa))
        s2.append(_dot_nt(k2, qb))
        v1.append(_top_values(s1[h], P_TOPK))
        v2.append(_top_values(s2[h], P_TOPK))
    r1 = [jnp.concatenate([v1[h][r] for h in range(P_HEADS)], axis=0) for r in range(P_TOPK)]
    r2 = [jnp.concatenate([v2[h][r] for h in range(P_HEADS)], axis=0) for r in range(P_TOPK)]
    pairs = _pair_candidates(P_TOPK)
    cand = [r1[i] + r2[j] for i, j in pairs]
    work = list(cand)
    thr = None
    for r in range(P_TOPK):
        thr = functools.reduce(jnp.maximum, work)
        if r + 1 < P_TOPK:
            work = [jnp.where(w == thr, -jnp.inf, w) for w in work]
    m1, m2 = r1[0], r2[0]
    z = functools.reduce(lambda a, b: a + b,
                         [jnp.where(c >= thr, jnp.exp(c - (m1 + m2)), 0.0) for c in cand])
    rz = 1.0 / z
    ea_top = [jnp.exp(r - m1) * rz for r in r1]
    eb_top = [jnp.exp(r - m2) for r in r2]
    tau = functools.reduce(jnp.minimum,
                           [jnp.where(c >= thr, ea_top[i] * eb_top[j], jnp.inf) for c, (i, j) in zip(cand, pairs)])
    tau_ref[...] = tau
    for h in range(P_HEADS):
        ea_ref[h] = jnp.exp(s1[h] - m1[h:h + 1, :]) * rz[h:h + 1, :]
        eb_ref[h] = jnp.exp(s2[h] - m2[h:h + 1, :])


def peer_select(xb, w_query, sub_keys, *, tn=256):
    t, d = xb.shape
    return pl.pallas_call(
        _peer_select_kernel,
        grid=(t // tn,),
        in_specs=[pl.BlockSpec((tn, d), lambda i: (i, 0)),
                  pl.BlockSpec((d, P_HEADS * P_QDIM), lambda i: (0, 0)),
                  pl.BlockSpec((2, P_NKEYS, P_HALF), lambda i: (0, 0, 0))],
        out_specs=[pl.BlockSpec((P_HEADS, P_NKEYS, tn), lambda i: (0, 0, i)),
                   pl.BlockSpec((P_HEADS, P_NKEYS, tn), lambda i: (0, 0, i)),
                   pl.BlockSpec((P_HEADS, tn), lambda i: (0, i))],
        out_shape=[jax.ShapeDtypeStruct((P_HEADS, P_NKEYS, t), F32),
                   jax.ShapeDtypeStruct((P_HEADS, P_NKEYS, t), F32),
                   jax.ShapeDtypeStruct((P_HEADS, t), F32)],
        compiler_params=_cparams(("parallel",)),
        name="peer_select",
    )(xb, w_query, sub_keys)


def _gelu_tanh(x):
    return 0.5 * x * (1.0 + jnp.tanh(math.sqrt(2.0 / math.pi) * (x + 0.044715 * (x * x * x))))


def _peer_ffn_kernel(xb_ref, u_ref, v_ref, ea_ref, eb_ref, tau_ref, x_ref, g_ref, b_ref,
                     o_ref, ob_ref, acc_ref, *, rows_per_step):
    e = pl.program_id(1)

    @pl.when(e == 0)
    def _():
        acc_ref[...] = jnp.zeros_like(acc_ref)

    act = _dot_nt(u_ref[...], xb_ref[...])
    gates = []
    for r in range(rows_per_step):
        a = e * rows_per_step + r
        g = None
        for h in range(P_HEADS):
            p = ea_ref[h, pl.ds(a, 1), :] * eb_ref[h]
            term = jnp.where(p >= tau_ref[h:h + 1, :], p, 0.0)
            g = term if g is None else g + term
        gates.append(g)
    hidden = (jnp.concatenate(gates, axis=0) * _gelu_tanh(act)).astype(BF16)
    acc_ref[...] += _dot_tn(hidden, v_ref[...])

    @pl.when(e == pl.num_programs(1) - 1)
    def _():
        y = _layer_norm(ALPHA * x_ref[...] + acc_ref[...], g_ref[...], b_ref[...])
        o_ref[...] = y
        ob_ref[...] = y.astype(BF16)


def peer_ffn_ln(xb, x, expert_u, expert_v, ea, eb, tau, gain, bias, *, tn=512, te=1024):
    t, d = x.shape
    n_e = expert_u.shape[0]
    tok = lambda i, e: (i, 0)
    once = pl.Buffered(1)
    return pl.pallas_call(
        functools.partial(_peer_ffn_kernel, rows_per_step=te // P_NKEYS),
        grid=(t // tn, n_e // te),
        in_specs=[pl.BlockSpec((tn, d), tok, pipeline_mode=once),
                  pl.BlockSpec((te, d), lambda i, e: (e, 0)),
                  pl.BlockSpec((te, d), lambda i, e: (e, 0)),
                  pl.BlockSpec((P_HEADS, P_NKEYS, tn), lambda i, e: (0, 0, i), pipeline_mode=once),
                  pl.BlockSpec((P_HEADS, P_NKEYS, tn), lambda i, e: (0, 0, i), pipeline_mode=once),
                  pl.BlockSpec((P_HEADS, tn), lambda i, e: (0, i)),
                  pl.BlockSpec((tn, d), tok, pipeline_mode=once),
                  pl.BlockSpec((1, d), lambda i, e: (0, 0)),
                  pl.BlockSpec((1, d), lambda i, e: (0, 0))],
        out_specs=[pl.BlockSpec((tn, d), tok), pl.BlockSpec((tn, d), tok)],
        out_shape=[jax.ShapeDtypeStruct((t, d), F32), jax.ShapeDtypeStruct((t, d), BF16)],
        scratch_shapes=[pltpu.VMEM((tn, d), F32)],
        compiler_params=_cparams(("parallel", "arbitrary"), PEER_VMEM_LIMIT),
        name="peer_ffn",
    )(xb, expert_u, expert_v, ea, eb, tau, x, gain.reshape(1, d), bias.reshape(1, d))


_ORIG_SIZES = (M_WIDTH, M_XBC, M_HEADS,
               H_HEADS * H_KEY, H_HEADS * H_KEY, H_HEADS * H_VAL, H_HEADS * H_VAL,
               DN_HEADS * DN_KEY, DN_HEADS * DN_KEY, DN_HEADS * DN_VAL, DN_HEADS, DN_HEADS, DN_HEADS * DN_VAL,
               G_HEADS * G_KEY, G_HEADS * G_KEY, G_HEADS * G_VAL, G_HEADS * G_VAL, G_GATE_RANK)
_NARROW = (2, 10, 11, 17)


def _split_w_in(w_in):
    offs = np.concatenate([[0], np.cumsum(_ORIG_SIZES)])
    wide = [w_in[:, :, offs[i]:offs[i + 1]] for i in range(len(_ORIG_SIZES)) if i not in _NARROW]
    narrow = [w_in[:, :, offs[i]:offs[i + 1]] for i in _NARROW]
    n_narrow = sum(_ORIG_SIZES[i] for i in _NARROW)
    pad = jnp.zeros(w_in.shape[:2] + (SMALL_WIDTH - n_narrow,), w_in.dtype)
    return jnp.concatenate(wide, axis=-1).astype(BF16), jnp.concatenate(narrow + [pad], axis=-1).astype(F32)


def kernel(x, w_in, m_conv_w, m_conv_b, m_dt_bias, m_a_log, m_d_skip, m_norm_w, h_lb_logits, h_norm_w, dn_conv_w, dn_a_log, dn_dt_bias, dn_norm_w, g_gate_w, g_gate_b, g_norm_w, w_branch, w_merge, b_merge, w_out, ln1_w, ln1_b, p_w_query, p_sub_keys, p_expert_u, p_expert_v, ln2_w, ln2_b):
    bsz, seq, d = x.shape
    t = bsz * seq
    w_big, w_small = _split_w_in(w_in)
    lb_p = jax.nn.softmax(h_lb_logits.astype(F32), axis=0)
    lower_bounds = jnp.cumsum(lb_p, axis=0) - lb_p[0]
    gate_w_pad = jnp.zeros((DEPTH, SMALL_WIDTH, G_HEADS * G_KEY), F32).at[:, S_LR:S_LR + G_GATE_RANK].set(g_gate_w.astype(F32))
    zero_bias = jnp.zeros((3 * DN_HEADS * DN_KEY,), F32)
    w_merge_b, w_branch_b, w_out_b = w_merge.astype(BF16), w_branch.astype(BF16), w_out.astype(BF16)
    wq_b, u_b, v_b = p_w_query.astype(BF16), p_expert_u.astype(BF16), p_expert_v.astype(BF16)

    xf = x.reshape(t, d).astype(F32)
    xb = xf.astype(BF16)
    for l in range(DEPTH):
        proj = matmul(xb, w_big[l], tm=512, tn=1024)
        small = matmul(xf, w_small[l], tm=512, tn=SMALL_WIDTH, precision=HIGHEST)
        small_t = small.T
        conv_m = conv_silu(proj, m_conv_w[l].astype(F32), m_conv_b[l].astype(F32), col0=C_XBC, seq=seq)
        conv_d = conv_silu(proj, dn_conv_w[l].astype(F32), zero_bias, col0=C_DQ, seq=seq)
        dt_t = small_t[S_DT:S_DT + M_HEADS].reshape(M_GROUPS, M_HEADS // M_GROUPS, t)
        y_m = ssd_branch(proj, conv_m, small, dt_t, m_dt_bias[l], m_a_log[l], m_d_skip[l], m_norm_w[l],
                         batch=bsz, seq=seq)
        y_h = hgrn2_branch(proj, lower_bounds[l], h_norm_w[l].astype(F32), batch=bsz, seq=seq)
        y_d = gdn_branch(proj, conv_d, small, small_t.reshape(SMALL_WIDTH, 1, t), dn_a_log[l], dn_dt_bias[l],
                         dn_norm_w[l], batch=bsz, seq=seq)
        y_g = gla_branch(proj, small, gate_w_pad[l], g_gate_b[l].astype(F32), g_norm_w[l].astype(F32),
                         batch=bsz, seq=seq)
        merged = merge_branches(xb, w_merge_b[l], b_merge[l].astype(F32), (y_m, y_h, y_d, y_g), w_branch_b[l])
        xf, xb = out_proj_ln(merged, w_out_b[l], xf, ln1_w[l].astype(F32), ln1_b[l].astype(F32))
        ea, eb, tau = peer_select(xb, wq_b[l], p_sub_keys[l].astype(F32))
        xf, xb = peer_ffn_ln(xb, xf, u_b[l], v_b[l], ea, eb, tau, ln2_w[l].astype(F32), ln2_b[l].astype(F32))
    return xf.reshape(bsz, seq, d).astype(x.dtype)
```

```python
import functools
import math

import numpy as np
import jax
import jax.numpy as jnp
from jax import lax
from jax.experimental import pallas as pl
from jax.experimental.pallas import tpu as pltpu

F32 = jnp.float32
BF16 = jnp.bfloat16
HIGHEST = lax.Precision.HIGHEST
NORM_EPS = 1e-5

D_MODEL = 2048
DEPTH = 4

M_HEADS, M_HEAD_DIM, M_GROUPS, M_STATE, M_CONV, M_CHUNK = 16, 64, 4, 128, 4, 64
M_WIDTH = M_HEADS * M_HEAD_DIM
M_XBC = M_WIDTH + 2 * M_GROUPS * M_STATE
M_GROUP_WIDTH = M_WIDTH // M_GROUPS

H_HEADS, H_KEY, H_VAL, H_CHUNK = 8, 128, 128, 16
DN_HEADS, DN_KEY, DN_VAL, DN_CONV, DN_CHUNK = 8, 128, 128, 4, 64
G_HEADS, G_KEY, G_VAL, G_GATE_RANK, G_GATE_TEMP, G_CHUNK = 4, 128, 256, 16, 16.0, 64
N_BRANCH, BRANCH_WIDTH = 4, 1024

P_HEADS, P_NKEYS, P_QDIM, P_TOPK = 8, 128, 256, 16
P_HALF = P_QDIM // 2
P_EXPERTS = P_NKEYS * P_NKEYS

ALPHA = (2 * DEPTH) ** 0.25

C_MZ, C_XBC = 0, 1024
C_HQ, C_HF, C_HI, C_HG = 3072, 4096, 5120, 6144
C_DQ, C_DG = 7168, 10240
C_GQ, C_GK, C_GV, C_GG = 11264, 11776, 12288, 13312
BIG_WIDTH = 14336
S_DT, S_DA, S_DB, S_LR = 0, 16, 24, 32
SMALL_WIDTH = 128
LANES = 128

VMEM_LIMIT = 48 * 1024 * 1024
PEER_VMEM_LIMIT = 56 * 1024 * 1024


def _cparams(sem, limit=None):
    return pltpu.CompilerParams(dimension_semantics=sem, vmem_limit_bytes=limit or VMEM_LIMIT)


def _dot(a, b, precision=None):
    return jnp.dot(a, b, preferred_element_type=F32, precision=precision)


def _dot_nt(a, b, precision=None):
    return lax.dot_general(a, b, (((1,), (1,)), ((), ())), preferred_element_type=F32, precision=precision)


def _dot_tn(a, b, precision=None):
    return lax.dot_general(a, b, (((0,), (0,)), ((), ())), preferred_element_type=F32, precision=precision)


def _sigmoid(x):
    return 1.0 / (1.0 + jnp.exp(-x))


def _silu(x):
    return x * _sigmoid(x)


def _softplus(x):
    return jnp.maximum(x, 0.0) + jnp.log(1.0 + jnp.exp(-jnp.abs(x)))


def _tril(n, strict=False):
    r = lax.broadcasted_iota(jnp.int32, (n, n), 0)
    c = lax.broadcasted_iota(jnp.int32, (n, n), 1)
    return (r > c) if strict else (r >= c)


def _matmul_kernel(x_ref, w_ref, o_ref, *, precision):
    o_ref[...] = _dot(x_ref[...], w_ref[...], precision).astype(o_ref.dtype)


def matmul(x, w, *, tm, tn, out_dtype=F32, precision=None):
    m, k = x.shape
    n = w.shape[1]
    return pl.pallas_call(
        functools.partial(_matmul_kernel, precision=precision),
        grid=(n // tn, m // tm),
        in_specs=[pl.BlockSpec((tm, k), lambda j, i: (i, 0)),
                  pl.BlockSpec((k, tn), lambda j, i: (0, j))],
        out_specs=pl.BlockSpec((tm, tn), lambda j, i: (i, j)),
        out_shape=jax.ShapeDtypeStruct((m, n), out_dtype),
        compiler_params=_cparams(("parallel", "parallel")),
        name="matmul",
    )(x, w)


def _conv_kernel(x_ref, halo_ref, w_ref, b_ref, o_ref, *, blocks_per_seq, taps):
    i = pl.program_id(0)
    first = (i % blocks_per_seq) == 0
    tb = x_ref.shape[0]
    halo = jnp.where(first, 0.0, halo_ref[...])
    ext = jnp.concatenate([halo, x_ref[...]], axis=0)
    acc = b_ref[...]
    for j in range(taps):
        off = 8 - (taps - 1) + j
        acc = acc + w_ref[j:j + 1, :] * ext[off:off + tb, :]
    o_ref[...] = _silu(acc).astype(o_ref.dtype)


def conv_silu(proj, w, b, *, col0, seq, tb=512, tc=1024, out_dtype=F32):
    t = proj.shape[0]
    taps, width = w.shape
    cb0 = col0 // tc
    hb = tb // 8
    return pl.pallas_call(
        functools.partial(_conv_kernel, blocks_per_seq=seq // tb, taps=taps),
        grid=(t // tb, width // tc),
        in_specs=[pl.BlockSpec((tb, tc), lambda i, j: (i, cb0 + j)),
                  pl.BlockSpec((8, tc), lambda i, j: (jnp.maximum(i * hb - 1, 0), cb0 + j)),
                  pl.BlockSpec((taps, tc), lambda i, j: (0, j)),
                  pl.BlockSpec((1, tc), lambda i, j: (0, j))],
        out_specs=pl.BlockSpec((tb, tc), lambda i, j: (i, j)),
        out_shape=jax.ShapeDtypeStruct((t, width), out_dtype),
        compiler_params=_cparams(("parallel", "parallel")),
        name="conv_silu",
    )(proj, proj, w, b.reshape(1, width))


def _gla_chunks(q, k, v, log_g, st_ref, chunk, emit):
    tb = q.shape[0]
    shift = int(math.log2(chunk))
    r = lax.broadcasted_iota(jnp.int32, (tb, tb), 0)
    c = lax.broadcasted_iota(jnp.int32, (tb, tb), 1)
    same = lax.shift_right_logical(r, shift) == lax.shift_right_logical(c, shift)
    incl = same & (r >= c)
    base = lax.shift_left(lax.shift_right_logical(r, shift), shift)
    tril_b = jnp.where(incl, 1.0, 0.0).astype(BF16)
    upto_mid = jnp.where(same & (c < base + chunk // 2), 1.0, 0.0).astype(BF16)
    whole = jnp.where(same, 1.0, 0.0).astype(BF16)
    b = _sel_dot(tril_b, log_g)
    b_mid = _sel_dot(upto_mid, log_g)
    b_end = _sel_dot(whole, log_g)
    qf = (q * jnp.exp(b - b_mid)).astype(BF16)
    kf = (k * jnp.exp(b_mid - b)).astype(BF16)
    att = jnp.where(incl, _dot_nt(qf, kf), 0.0).astype(BF16)
    vb = v.astype(BF16)
    o_intra = _dot(att, vb)
    qd = (q * jnp.exp(b)).astype(BF16)
    k_end = (k * jnp.exp(b_end - b)).astype(BF16)
    g_end = jnp.exp(b_end)
    st = st_ref[...]
    for ci in range(tb // chunk):
        rows = slice(ci * chunk, (ci + 1) * chunk)
        emit(ci, o_intra[rows] + _dot_nt(qd[rows], st.astype(BF16)))
        st = st * g_end[ci * chunk:ci * chunk + 1, :] + _dot_tn(vb[rows], k_end[rows])
    st_ref[...] = st


def _rms(x):
    return x * lax.rsqrt(jnp.mean(x * x, axis=-1, keepdims=True) + NORM_EPS)


def _hgrn2_kernel(q_ref, f_ref, i_ref, g_ref, lb_ref, nw_ref, o_ref, st_ref, *, chunk):
    @pl.when(pl.program_id(2) == 0)
    def _():
        st_ref[...] = jnp.zeros_like(st_ref)

    lb = lb_ref[...]
    pre = f_ref[...]
    forget = lb + (1.0 - lb) * _sigmoid(pre)
    k = (1.0 - lb) * _sigmoid(-pre)
    gate = _sigmoid(g_ref[...])
    nw = nw_ref[...]

    def emit(c, o):
        rows = slice(c * chunk, (c + 1) * chunk)
        o_ref[rows, :] = (_rms(o) * nw * gate[rows]).astype(o_ref.dtype)

    _gla_chunks(_silu(q_ref[...]), k, i_ref[...], jnp.log(forget), st_ref, chunk, emit)


def hgrn2_branch(proj, lower_bound, norm_w, *, batch, seq, tb=256, out_dtype=BF16):
    t = proj.shape[0]
    nb = seq // tb
    kb = H_KEY // LANES

    def col(c0):
        return pl.BlockSpec((tb, H_KEY), lambda b, h, i: (b * nb + i, c0 // H_KEY + h))

    return pl.pallas_call(
        functools.partial(_hgrn2_kernel, chunk=H_CHUNK),
        grid=(batch, H_HEADS, nb),
        in_specs=[col(C_HQ), col(C_HF), col(C_HI), col(C_HG),
                  pl.BlockSpec((1, H_KEY), lambda b, h, i: (0, h)),
                  pl.BlockSpec((1, H_VAL), lambda b, h, i: (0, 0))],
        out_specs=pl.BlockSpec((tb, H_VAL), lambda b, h, i: (b * nb + i, h)),
        out_shape=jax.ShapeDtypeStruct((t, H_HEADS * H_VAL), out_dtype),
        scratch_shapes=[pltpu.VMEM((H_VAL, H_KEY), F32)],
        compiler_params=_cparams(("parallel", "parallel", "arbitrary")),
        name="hgrn2",
    )(proj, proj, proj, proj, lower_bound.reshape(1, -1), norm_w.reshape(1, -1))


def _gla_kernel(q_ref, k_ref, v_ref, g_ref, lr_ref, gw_ref, gb_ref, nw_ref, o_ref, st_ref, *, chunk):
    @pl.when(pl.program_id(2) == 0)
    def _():
        st_ref[...] = jnp.zeros_like(st_ref)

    z = _dot(lr_ref[...], gw_ref[...], HIGHEST) + gb_ref[...]
    log_a = (jnp.minimum(z, 0.0) - jnp.log(1.0 + jnp.exp(-jnp.abs(z)))) * (1.0 / G_GATE_TEMP)
    gate = _silu(g_ref[...])
    nw = nw_ref[...]

    def emit(c, o):
        rows = slice(c * chunk, (c + 1) * chunk)
        o_ref[rows, :] = (_rms(o) * nw * gate[rows]).astype(o_ref.dtype)

    _gla_chunks(q_ref[...] * G_KEY ** -0.5, k_ref[...], v_ref[...], log_a, st_ref, chunk, emit)


def gla_branch(proj, small, gate_w_pad, gate_b, norm_w, *, batch, seq, tb=256, out_dtype=BF16):
    t = proj.shape[0]
    nb = seq // tb

    def col(c0, w):
        return pl.BlockSpec((tb, w), lambda b, h, i: (b * nb + i, c0 // w + h))

    return pl.pallas_call(
        functools.partial(_gla_kernel, chunk=G_CHUNK),
        grid=(batch, G_HEADS, nb),
        in_specs=[col(C_GQ, G_KEY), col(C_GK, G_KEY), col(C_GV, G_VAL), col(C_GG, G_VAL),
                  pl.BlockSpec((tb, SMALL_WIDTH), lambda b, h, i: (b * nb + i, 0)),
                  pl.BlockSpec((SMALL_WIDTH, G_KEY), lambda b, h, i: (0, h)),
                  pl.BlockSpec((1, G_KEY), lambda b, h, i: (0, h)),
                  pl.BlockSpec((1, G_VAL), lambda b, h, i: (0, 0))],
        out_specs=pl.BlockSpec((tb, G_VAL), lambda b, h, i: (b * nb + i, h)),
        out_shape=jax.ShapeDtypeStruct((t, G_HEADS * G_VAL), out_dtype),
        scratch_shapes=[pltpu.VMEM((G_VAL, G_KEY), F32)],
        compiler_params=_cparams(("parallel", "parallel", "arbitrary")),
        name="gla",
    )(proj, proj, proj, proj, small, gate_w_pad, gate_b.reshape(1, -1), norm_w.reshape(1, -1))


def _ssd_kernel(z_ref, x_ref, b_ref, c_ref, small_ref, dtt_ref, dtb_ref, dtbc_ref, a_ref, ac_ref,
                e_ref, dskip_ref, nw_ref, o_ref, h_ref, *, chunk):
    @pl.when(pl.program_id(2) == 0)
    def _():
        h_ref[...] = jnp.zeros_like(h_ref)

    tb = x_ref.shape[0]
    hg = M_HEADS // M_GROUPS
    p = M_HEAD_DIM
    tril = _tril(chunk)
    tril_f = tril.astype(F32)
    triu_f = (lax.broadcasted_iota(jnp.int32, (chunk, chunk), 0)
              <= lax.broadcasted_iota(jnp.int32, (chunk, chunk), 1)).astype(F32)
    dt = _dot(_softplus(small_ref[...] + dtb_ref[...]), e_ref[...], HIGHEST)
    da = dt * a_ref[...]
    da_t = _softplus(dtt_ref[...] + dtbc_ref[...]) * ac_ref[...]
    x = x_ref[...]
    xdt = x * dt
    zg = _silu(z_ref[...])
    bm = b_ref[...].astype(BF16)
    cm = c_ref[...].astype(BF16)
    dskip = dskip_ref[...]
    nw = nw_ref[...]
    for c in range(tb // chunk):
        rows = slice(c * chunk, (c + 1) * chunk)
        acs = _dot(tril_f, da[rows], HIGHEST)
        acs_t = _dot(da_t[:, rows], triu_f, HIGHEST)
        cb = _dot_nt(cm[rows], bm[rows])
        h_all = h_ref[...]
        y = _dot_nt(cm[rows], h_all.astype(BF16)) * jnp.exp(acs)
        xdt_c = xdt[rows]
        intra = []
        for h in range(hg):
            col = acs[:, h * p:h * p + 1]
            row = acs_t[h:h + 1, :]
            seg = jnp.where(tril, jnp.exp(jnp.minimum(col - row, 0.0)), 0.0)
            intra.append(_dot((cb * seg).astype(BF16), xdt_c[:, h * p:(h + 1) * p].astype(BF16)))
        y = y + jnp.concatenate(intra, axis=1)
        y = y + x[rows] * dskip
        y = y * zg[rows]
        o_ref[rows, :] = (_rms(y) * nw).astype(o_ref.dtype)
        acs_end = acs[chunk - 1:chunk, :]
        states = _dot_tn((xdt_c * jnp.exp(acs_end - acs)).astype(BF16), bm[rows])
        for h in range(hg):
            dec = jnp.exp(acs_t[h:h + 1, chunk - 1:chunk])
            hs = slice(h * p, (h + 1) * p)
            h_ref[hs, :] = h_all[hs] * dec + states[hs]


def ssd_branch(proj, conv, small, dt_t, dt_bias, a_log, d_skip, norm_w, *, batch, seq, tb=256, out_dtype=BF16):
    t = proj.shape[0]
    nb = seq // tb
    hg = M_HEADS // M_GROUPS
    gw = M_GROUP_WIDTH
    rep = lambda v: jnp.repeat(v.astype(F32), M_HEAD_DIM).reshape(1, M_WIDTH)
    a = -jnp.exp(a_log.astype(F32))
    expand = np.zeros((M_GROUPS, SMALL_WIDTH, gw), np.float32)
    for g in range(M_GROUPS):
        for h in range(hg):
            expand[g, S_DT + g * hg + h, h * M_HEAD_DIM:(h + 1) * M_HEAD_DIM] = 1.0
    dtb_pad = jnp.zeros((1, SMALL_WIDTH), F32).at[0, S_DT:S_DT + M_HEADS].set(dt_bias.astype(F32))
    row = lambda i_of: pl.BlockSpec((tb, gw), i_of)
    return pl.pallas_call(
        functools.partial(_ssd_kernel, chunk=M_CHUNK),
        grid=(batch, M_GROUPS, nb),
        in_specs=[pl.BlockSpec((tb, gw), lambda b, g, i: (b * nb + i, g)),
                  pl.BlockSpec((tb, gw), lambda b, g, i: (b * nb + i, g)),
                  pl.BlockSpec((tb, M_STATE), lambda b, g, i: (b * nb + i, M_WIDTH // M_STATE + g)),
                  pl.BlockSpec((tb, M_STATE), lambda b, g, i: (b * nb + i, (M_WIDTH + M_GROUPS * M_STATE) // M_STATE + g)),
                  pl.BlockSpec((tb, SMALL_WIDTH), lambda b, g, i: (b * nb + i, 0)),
                  pl.BlockSpec((None, hg, tb), lambda b, g, i: (g, 0, b * nb + i)),
                  pl.BlockSpec((1, SMALL_WIDTH), lambda b, g, i: (0, 0)),
                  pl.BlockSpec((None, hg, 1), lambda b, g, i: (g, 0, 0)),
                  pl.BlockSpec((1, gw), lambda b, g, i: (0, g)),
                  pl.BlockSpec((None, hg, 1), lambda b, g, i: (g, 0, 0)),
                  pl.BlockSpec((None, SMALL_WIDTH, gw), lambda b, g, i: (g, 0, 0)),
                  pl.BlockSpec((1, gw), lambda b, g, i: (0, g)),
                  pl.BlockSpec((1, gw), lambda b, g, i: (0, g))],
        out_specs=pl.BlockSpec((tb, gw), lambda b, g, i: (b * nb + i, g)),
        out_shape=jax.ShapeDtypeStruct((t, M_WIDTH), out_dtype),
        scratch_shapes=[pltpu.VMEM((gw, M_STATE), F32)],
        compiler_params=_cparams(("parallel", "parallel", "arbitrary")),
        name="ssd",
    )(proj, conv, conv, conv, small, dt_t, dtb_pad, dt_bias.astype(F32).reshape(M_GROUPS, hg, 1),
      rep(a), a.reshape(M_GROUPS, hg, 1), jnp.asarray(expand), rep(d_skip), norm_w.astype(F32).reshape(1, M_WIDTH))


def _split3(x):
    hi = x.astype(BF16)
    r = x - hi.astype(F32)
    mid = r.astype(BF16)
    lo = (r - mid.astype(F32)).astype(BF16)
    return hi, mid, lo


def _dot_sel(x, sel):
    hi, mid, lo = _split3(x)
    return _dot(hi, sel) + _dot(mid, sel) + _dot(lo, sel)


def _sel_dot(sel, x):
    hi, mid, lo = _split3(x)
    return _dot(sel, hi) + _dot(sel, mid) + _dot(sel, lo)


def _gdn_kernel(q_ref, k_ref, v_ref, g_ref, small_ref, at_ref, sel_ref, alog_ref, dtb_ref, nw_ref,
                o_ref, s_ref, *, chunk):
    @pl.when(pl.program_id(2) == 0)
    def _():
        s_ref[...] = jnp.zeros_like(s_ref)

    tb = q_ref.shape[0]
    dk = DN_KEY
    shift = int(math.log2(chunk))
    r = lax.broadcasted_iota(jnp.int32, (tb, tb), 0)
    c = lax.broadcasted_iota(jnp.int32, (tb, tb), 1)
    same = lax.shift_right_logical(r, shift) == lax.shift_right_logical(c, shift)
    incl = same & (r >= c)
    strict = same & (r > c)
    tril_b = jnp.where(incl, 1.0, 0.0).astype(BF16)
    triu_b = jnp.where(same & (r <= c), 1.0, 0.0).astype(BF16)

    ab = _dot_sel(small_ref[...], sel_ref[...])
    neg_a = -jnp.exp(alog_ref[...])
    la = neg_a * _softplus(ab[:, :dk] + dtb_ref[...])
    beta = _sigmoid(ab[:, dk:])
    la_t = neg_a[:, 0:1] * _softplus(at_ref[...] + dtb_ref[:, 0:1])
    dec = _sel_dot(tril_b, la)
    dec_t = _dot_sel(jnp.broadcast_to(la_t, (8, tb)), triu_b)[0:1, :]
    dec_wide = jnp.concatenate([dec] * (tb // dk), axis=1)
    lmask = jnp.where(incl, jnp.exp(jnp.minimum(dec_wide - dec_t, 0.0)), 0.0)

    q = q_ref[...]
    k = k_ref[...]
    q = q * lax.rsqrt(jnp.sum(q * q, axis=-1, keepdims=True) + 1e-6) * dk ** -0.5
    k = k * lax.rsqrt(jnp.sum(k * k, axis=-1, keepdims=True) + 1e-6)
    kb = k * beta
    kbf = k.astype(BF16)
    e_dec = jnp.exp(dec)
    pw = jnp.where(strict, -(_dot_nt(kb.astype(BF16), kbf) * lmask), 0.0)
    y = jnp.concatenate([kb * e_dec, v_ref[...] * beta], axis=1)
    for n in range(shift):
        if n > 0:
            pwb = pw.astype(BF16)
            pw = _dot(pwb, pwb)
        y = y + _dot(pw.astype(BF16), y.astype(BF16))
    yb = y.astype(BF16)
    att = (_dot_nt(q.astype(BF16), kbf) * lmask).astype(BF16)
    av = _dot(att, yb)
    qp = (q * e_dec - av[:, :dk]).astype(BF16)
    o_intra = av[:, dk:]
    gate = _silu(g_ref[...])
    nw = nw_ref[...]
    s = s_ref[...]
    for ci in range(tb // chunk):
        rows = slice(ci * chunk, (ci + 1) * chunk)
        last = (ci + 1) * chunk - 1
        k_end = (k[rows] * jnp.exp(dec[last:last + 1, :] - dec[rows])).astype(BF16)
        wn = _dot_tn(k_end, yb[rows])
        sb = s.astype(BF16)
        o = _dot(qp[rows], sb) + o_intra[rows]
        o_ref[rows, :] = (_rms(o) * nw * gate[rows]).astype(o_ref.dtype)
        s = s * jnp.exp(dec_t[:, last:last + 1]) - _dot(wn[:, :dk].astype(BF16), sb) + wn[:, dk:]
    s_ref[...] = s


def gdn_branch(proj, conv, small, small_t, a_log, dt_bias, norm_w, *, batch, seq, tb=256, out_dtype=BF16):
    t = proj.shape[0]
    nb = seq // tb
    sel = np.zeros((DN_HEADS, SMALL_WIDTH, 2 * DN_KEY), np.float32)
    for h in range(DN_HEADS):
        sel[h, S_DA + h, :DN_KEY] = 1.0
        sel[h, S_DB + h, DN_KEY:] = 1.0
    rep = lambda v: jnp.broadcast_to(v.astype(F32).reshape(DN_HEADS, 1, 1), (DN_HEADS, 1, DN_KEY))
    hd = DN_HEADS

    def col(off):
        return pl.BlockSpec((tb, DN_KEY), lambda b, h, i: (b * nb + i, off + h))

    return pl.pallas_call(
        functools.partial(_gdn_kernel, chunk=DN_CHUNK),
        grid=(batch, DN_HEADS, nb),
        in_specs=[col(0), col(hd), col(2 * hd), col(C_DG // DN_VAL),
                  pl.BlockSpec((tb, SMALL_WIDTH), lambda b, h, i: (b * nb + i, 0)),
                  pl.BlockSpec((None, 1, tb), lambda b, h, i: (S_DA + h, 0, b * nb + i)),
                  pl.BlockSpec((None, SMALL_WIDTH, 2 * DN_KEY), lambda b, h, i: (h, 0, 0)),
                  pl.BlockSpec((None, 1, DN_KEY), lambda b, h, i: (h, 0, 0)),
                  pl.BlockSpec((None, 1, DN_KEY), lambda b, h, i: (h, 0, 0)),
                  pl.BlockSpec((1, DN_VAL), lambda b, h, i: (0, 0))],
        out_specs=pl.BlockSpec((tb, DN_VAL), lambda b, h, i: (b * nb + i, h)),
        out_shape=jax.ShapeDtypeStruct((t, DN_HEADS * DN_VAL), out_dtype),
        scratch_shapes=[pltpu.VMEM((DN_KEY, DN_VAL), F32)],
        compiler_params=_cparams(("parallel", "parallel", "arbitrary")),
        name="gdn",
    )(conv, conv, conv, proj, small, small_t, jnp.asarray(sel, BF16), rep(a_log), rep(dt_bias),
      norm_w.astype(F32).reshape(1, DN_VAL))


def _layer_norm(y, gain, bias):
    yc = y - jnp.mean(y, axis=-1, keepdims=True)
    return yc * lax.rsqrt(jnp.mean(yc * yc, axis=-1, keepdims=True) + NORM_EPS) * gain + bias


def _merge_kernel(x_ref, *refs):
    wm, bm, ys, wb, o_ref = refs[0:4], refs[4:8], refs[8:12], refs[12:16], refs[16]
    x = x_ref[...]
    acc = None
    for n in range(N_BRANCH):
        gate = _sigmoid(_dot(x, wm[n][...]) + bm[n][...])
        term = gate * _dot(ys[n][...], wb[n][...])
        acc = term if acc is None else acc + term
    o_ref[...] = acc.astype(o_ref.dtype)


def merge_branches(xb, w_merge, b_merge, ys, w_branch, *, tm=512, tn=512):
    t, d = xb.shape
    nj = d // tn
    wm_specs = [pl.BlockSpec((d, tn), functools.partial(lambda j, i, n: (0, n * nj + j), n=n)) for n in range(N_BRANCH)]
    bm_specs = [pl.BlockSpec((1, tn), functools.partial(lambda j, i, n: (0, n * nj + j), n=n)) for n in range(N_BRANCH)]
    y_specs = [pl.BlockSpec((tm, BRANCH_WIDTH), lambda j, i: (i, 0)) for _ in range(N_BRANCH)]
    wb_specs = [pl.BlockSpec((None, BRANCH_WIDTH, tn), functools.partial(lambda j, i, n: (n, 0, j), n=n)) for n in range(N_BRANCH)]
    return pl.pallas_call(
        _merge_kernel,
        grid=(nj, t // tm),
        in_specs=[pl.BlockSpec((tm, d), lambda j, i: (i, 0))] + wm_specs + bm_specs + y_specs + wb_specs,
        out_specs=pl.BlockSpec((tm, tn), lambda j, i: (i, j)),
        out_shape=jax.ShapeDtypeStruct((t, d), BF16),
        compiler_params=_cparams(("parallel", "parallel")),
        name="merge",
    )(xb, *([w_merge] * N_BRANCH), *([b_merge.reshape(1, -1)] * N_BRANCH), *ys, *([w_branch] * N_BRANCH))


def _out_ln_kernel(m_ref, w_ref, x_ref, g_ref, b_ref, o_ref, ob_ref):
    y = ALPHA * x_ref[...] + _dot(m_ref[...], w_ref[...])
    y = _layer_norm(y, g_ref[...], b_ref[...])
    o_ref[...] = y
    ob_ref[...] = y.astype(BF16)


def out_proj_ln(merged, w_out, x, gain, bias, *, tm=512):
    t, d = x.shape
    return pl.pallas_call(
        _out_ln_kernel,
        grid=(t // tm,),
        in_specs=[pl.BlockSpec((tm, d), lambda i: (i, 0)),
                  pl.BlockSpec((d, d), lambda i: (0, 0)),
                  pl.BlockSpec((tm, d), lambda i: (i, 0)),
                  pl.BlockSpec((1, d), lambda i: (0, 0)),
                  pl.BlockSpec((1, d), lambda i: (0, 0))],
        out_specs=[pl.BlockSpec((tm, d), lambda i: (i, 0)), pl.BlockSpec((tm, d), lambda i: (i, 0))],
        out_shape=[jax.ShapeDtypeStruct((t, d), F32), jax.ShapeDtypeStruct((t, d), BF16)],
        compiler_params=_cparams(("parallel",)),
        name="out_proj_ln",
    )(merged, w_out, x, gain.reshape(1, d), bias.reshape(1, d))


def _pair_candidates(k):
    return [(i, j) for i in range(k) for j in range(k) if (i + 1) * (j + 1) <= k]


def _top_values(s, k):
    out = []
    for r in range(k):
        m = jnp.max(s, axis=0, keepdims=True)
        out.append(m)
        if r + 1 < k:
            s = jnp.where(s == m, -jnp.inf, s)
    return out


def _peer_select_kernel(x_ref, wq_ref, keys_ref, ea_ref, eb_ref, tau_ref):
    q = _dot(x_ref[...], wq_ref[...])
    k1 = keys_ref[0].astype(BF16)
    k2 = keys_ref[1].astype(BF16)
    s1, s2, v1, v2 = [], [], [], []
    for h in range(P_HEADS):
        qa = q[:, (2 * h) * P_HALF:(2 * h + 1) * P_HALF].astype(BF16)
        qb = q[:, (2 * h + 1) * P_HALF:(2 * h + 2) * P_HALF].astype(BF16)
        s1.append(_dot_nt(k1, qa))
        s2.append(_dot_nt(k2, qb))
        v1.append(_top_values(s1[h], P_TOPK))
        v2.append(_top_values(s2[h], P_TOPK))
    r1 = [jnp.concatenate([v1[h][r] for h in range(P_HEADS)], axis=0) for r in range(P_TOPK)]
    r2 = [jnp.concatenate([v2[h][r] for h in range(P_HEADS)], axis=0) for r in range(P_TOPK)]
    pairs = _pair_candidates(P_TOPK)
    cand = [r1[i] + r2[j] for i, j in pairs]
    work = list(cand)
    thr = None
    for r in range(P_TOPK):
        thr = functools.reduce(jnp.maximum, work)
        if r + 1 < P_TOPK:
            work = [jnp.where(w == thr, -jnp.inf, w) for w in work]
    m1, m2 = r1[0], r2[0]
    z = functools.reduce(lambda a, b: a + b,
                         [jnp.where(c >= thr, jnp.exp(c - (m1 + m2)), 0.0) for c in cand])
    rz = 1.0 / z
    ea_top = [jnp.exp(r - m1) * rz for r in r1]
    eb_top = [jnp.exp(r - m2) for r in r2]
    tau = functools.reduce(jnp.minimum,
                           [jnp.where(c >= thr, ea_top[i] * eb_top[j], jnp.inf) for c, (i, j) in zip(cand, pairs)])
    tau_ref[...] = tau
    for h in range(P_HEADS):
        ea_ref[h] = jnp.exp(s1[h] - m1[h:h + 1, :]) * rz[h:h + 1, :]
        eb_ref[h] = jnp.exp(s2[h] - m2[h:h + 1, :])


def peer_select(xb, w_query, sub_keys, *, tn=256):
    t, d = xb.shape
    return pl.pallas_call(
        _peer_select_kernel,
        grid=(t // tn,),
        in_specs=[pl.BlockSpec((tn, d), lambda i: (i, 0)),
                  pl.BlockSpec((d, P_HEADS * P_QDIM), lambda i: (0, 0)),
                  pl.BlockSpec((2, P_NKEYS, P_HALF), lambda i: (0, 0, 0))],
        out_specs=[pl.BlockSpec((P_HEADS, P_NKEYS, tn), lambda i: (0, 0, i)),
                   pl.BlockSpec((P_HEADS, P_NKEYS, tn), lambda i: (0, 0, i)),
                   pl.BlockSpec((P_HEADS, tn), lambda i: (0, i))],
        out_shape=[jax.ShapeDtypeStruct((P_HEADS, P_NKEYS, t), F32),
                   jax.ShapeDtypeStruct((P_HEADS, P_NKEYS, t), F32),
                   jax.ShapeDtypeStruct((P_HEADS, t), F32)],
        compiler_params=_cparams(("parallel",)),
        name="peer_select",
    )(xb, w_query, sub_keys)


def _gelu_tanh(x):
    return 0.5 * x * (1.0 + jnp.tanh(math.sqrt(2.0 / math.pi) * (x + 0.044715 * (x * x * x))))


def _peer_ffn_kernel(xb_ref, u_ref, v_ref, ea_ref, eb_ref, tau_ref, x_ref, g_ref, b_ref,
                     o_ref, ob_ref, acc_ref, *, rows_per_step):
    e = pl.program_id(1)

    @pl.when(e == 0)
    def _():
        acc_ref[...] = jnp.zeros_like(acc_ref)

    act = _dot_nt(u_ref[...], xb_ref[...])
    gates = []
    for r in range(rows_per_step):
        a = e * rows_per_step + r
        g = None
        for h in range(P_HEADS):
            p = ea_ref[h, pl.ds(a, 1), :] * eb_ref[h]
            term = jnp.where(p >= tau_ref[h:h + 1, :], p, 0.0)
            g = term if g is None else g + term
        gates.append(g)
    hidden = (jnp.concatenate(gates, axis=0) * _gelu_tanh(act)).astype(BF16)
    acc_ref[...] += _dot_tn(hidden, v_ref[...])

    @pl.when(e == pl.num_programs(1) - 1)
    def _():
        y = _layer_norm(ALPHA * x_ref[...] + acc_ref[...], g_ref[...], b_ref[...])
        o_ref[...] = y
        ob_ref[...] = y.astype(BF16)


def peer_ffn_ln(xb, x, expert_u, expert_v, ea, eb, tau, gain, bias, *, tn=512, te=1024):
    t, d = x.shape
    n_e = expert_u.shape[0]
    tok = lambda i, e: (i, 0)
    once = pl.Buffered(1)
    return pl.pallas_call(
        functools.partial(_peer_ffn_kernel, rows_per_step=te // P_NKEYS),
        grid=(t // tn, n_e // te),
        in_specs=[pl.BlockSpec((tn, d), tok, pipeline_mode=once),
                  pl.BlockSpec((te, d), lambda i, e: (e, 0)),
                  pl.BlockSpec((te, d), lambda i, e: (e, 0)),
                  pl.BlockSpec((P_HEADS, P_NKEYS, tn), lambda i, e: (0, 0, i), pipeline_mode=once),
                  pl.BlockSpec((P_HEADS, P_NKEYS, tn), lambda i, e: (0, 0, i), pipeline_mode=once),
                  pl.BlockSpec((P_HEADS, tn), lambda i, e: (0, i)),
                  pl.BlockSpec((tn, d), tok, pipeline_mode=once),
                  pl.BlockSpec((1, d), lambda i, e: (0, 0)),
                  pl.BlockSpec((1, d), lambda i, e: (0, 0))],
        out_specs=[pl.BlockSpec((tn, d), tok), pl.BlockSpec((tn, d), tok)],
        out_shape=[jax.ShapeDtypeStruct((t, d), F32), jax.ShapeDtypeStruct((t, d), BF16)],
        scratch_shapes=[pltpu.VMEM((tn, d), F32)],
        compiler_params=_cparams(("parallel", "arbitrary"), PEER_VMEM_LIMIT),
        name="peer_ffn",
    )(xb, expert_u, expert_v, ea, eb, tau, x, gain.reshape(1, d), bias.reshape(1, d))


_ORIG_SIZES = (M_WIDTH, M_XBC, M_HEADS,
               H_HEADS * H_KEY, H_HEADS * H_KEY, H_HEADS * H_VAL, H_HEADS * H_VAL,
               DN_HEADS * DN_KEY, DN_HEADS * DN_KEY, DN_HEADS * DN_VAL, DN_HEADS, DN_HEADS, DN_HEADS * DN_VAL,
               G_HEADS * G_KEY, G_HEADS * G_KEY, G_HEADS * G_VAL, G_HEADS * G_VAL, G_GATE_RANK)
_NARROW = (2, 10, 11, 17)


def _split_w_in(w_in):
    offs = np.concatenate([[0], np.cumsum(_ORIG_SIZES)])
    wide = [w_in[:, :, offs[i]:offs[i + 1]] for i in range(len(_ORIG_SIZES)) if i not in _NARROW]
    narrow = [w_in[:, :, offs[i]:offs[i + 1]] for i in _NARROW]
    n_narrow = sum(_ORIG_SIZES[i] for i in _NARROW)
    pad = jnp.zeros(w_in.shape[:2] + (SMALL_WIDTH - n_narrow,), w_in.dtype)
    return jnp.concatenate(wide, axis=-1).astype(BF16), jnp.concatenate(narrow + [pad], axis=-1).astype(F32)


def kernel(x, w_in, m_conv_w, m_conv_b, m_dt_bias, m_a_log, m_d_skip, m_norm_w, h_lb_logits, h_norm_w, dn_conv_w, dn_a_log, dn_dt_bias, dn_norm_w, g_gate_w, g_gate_b, g_norm_w, w_branch, w_merge, b_merge, w_out, ln1_w, ln1_b, p_w_query, p_sub_keys, p_expert_u, p_expert_v, ln2_w, ln2_b):
    bsz, seq, d = x.shape
    t = bsz * seq
    w_big, w_small = _split_w_in(w_in)
    lb_p = jax.nn.softmax(h_lb_logits.astype(F32), axis=0)
    lower_bounds = jnp.cumsum(lb_p, axis=0) - lb_p[0]
    gate_w_pad = jnp.zeros((DEPTH, SMALL_WIDTH, G_HEADS * G_KEY), F32).at[:, S_LR:S_LR + G_GATE_RANK].set(g_gate_w.astype(F32))
    zero_bias = jnp.zeros((3 * DN_HEADS * DN_KEY,), F32)
    w_merge_b, w_branch_b, w_out_b = w_merge.astype(BF16), w_branch.astype(BF16), w_out.astype(BF16)
    wq_b, u_b, v_b = p_w_query.astype(BF16), p_expert_u.astype(BF16), p_expert_v.astype(BF16)

    xf = x.reshape(t, d).astype(F32)
    xb = xf.astype(BF16)
    for l in range(DEPTH):
        proj = matmul(xb, w_big[l], tm=1024, tn=1024)
        small = matmul(xf, w_small[l], tm=512, tn=SMALL_WIDTH, precision=HIGHEST)
        small_t = small.T
        conv_m = conv_silu(proj, m_conv_w[l].astype(F32), m_conv_b[l].astype(F32), col0=C_XBC, seq=seq)
        conv_d = conv_silu(proj, dn_conv_w[l].astype(F32), zero_bias, col0=C_DQ, seq=seq)
        dt_t = small_t[S_DT:S_DT + M_HEADS].reshape(M_GROUPS, M_HEADS // M_GROUPS, t)
        y_m = ssd_branch(proj, conv_m, small, dt_t, m_dt_bias[l], m_a_log[l], m_d_skip[l], m_norm_w[l],
                         batch=bsz, seq=seq)
        y_h = hgrn2_branch(proj, lower_bounds[l], h_norm_w[l].astype(F32), batch=bsz, seq=seq)
        y_d = gdn_branch(proj, conv_d, small, small_t.reshape(SMALL_WIDTH, 1, t), dn_a_log[l], dn_dt_bias[l],
                         dn_norm_w[l], batch=bsz, seq=seq)
        y_g = gla_branch(proj, small, gate_w_pad[l], g_gate_b[l].astype(F32), g_norm_w[l].astype(F32),
                         batch=bsz, seq=seq)
        merged = merge_branches(xb, w_merge_b[l], b_merge[l].astype(F32), (y_m, y_h, y_d, y_g), w_branch_b[l])
        xf, xb = out_proj_ln(merged, w_out_b[l], xf, ln1_w[l].astype(F32), ln1_b[l].astype(F32))
        ea, eb, tau = peer_select(xb, wq_b[l], p_sub_keys[l].astype(F32))
        xf, xb = peer_ffn_ln(xb, xf, u_b[l], v_b[l], ea, eb, tau, ln2_w[l].astype(F32), ln2_b[l].astype(F32))
    return xf.reshape(bsz, seq, d).astype(x.dtype)
```
